```python
import jax, jax.numpy as jnp
from jax import lax
import numpy as np

D_MODEL = 1024
BATCH = 8
SEQ = 4096
DEPTH = 4

N_META = 16
ATTN_HEADS = 8
ATTN_KV_HEADS = 2
HEAD_DIM = 64
WINDOW = 128
BLOCK = 128
ATTN_WIDTH = ATTN_HEADS * HEAD_DIM
KV_WIDTH = ATTN_KV_HEADS * HEAD_DIM
SSM_HEAD_DIM = 64
SSM_HEADS = 24
SSM_D_INNER = SSM_HEADS * SSM_HEAD_DIM
SSM_GROUPS = 2
SSM_STATE = 128
CONV_K = 4
CHUNK = 128
CONV_DIM = SSM_D_INNER + 2 * SSM_GROUPS * SSM_STATE
MIX_WIDTH = ATTN_WIDTH + SSM_D_INNER
IN_PROJ = ATTN_WIDTH + 2 * KV_WIDTH + SSM_D_INNER + CONV_DIM + SSM_HEADS
D_FF = 3584
N_EXPERTS = 8
TOP_K = 2
N_DENSE = (DEPTH + 1) // 2
N_MOE = DEPTH // 2
EPS = 1e-6

kernel_name = "hymba_ssd_swa_sink_moe_trunk"


def rmsnorm(x, g):
    xf = x.astype(jnp.float32)
    y = xf * lax.rsqrt(jnp.mean(xf * xf, axis=-1, keepdims=True) + EPS)
    return (y * g.astype(jnp.float32)).astype(x.dtype)


def causal_dwconv(u, w, b):
    rhs = jnp.transpose(w)[:, None, :].astype(u.dtype)
    y = lax.conv_general_dilated(u, rhs, window_strides=(1,), padding=[(CONV_K - 1, 0)],
                                 dimension_numbers=("NWC", "WIO", "NWC"),
                                 feature_group_count=u.shape[-1])
    return y + b.astype(u.dtype)


def ssd_scan(xs, dt, a, bm, cm):
    f32 = jnp.float32
    bsz, lp = xs.shape[:2]
    nc = lp // CHUNK
    hg = SSM_HEADS // SSM_GROUPS
    xdt = (xs.astype(f32) * dt[..., None]).reshape(bsz, nc, CHUNK, SSM_GROUPS, hg, SSM_HEAD_DIM)
    bc = bm.astype(f32).reshape(bsz, nc, CHUNK, SSM_GROUPS, SSM_STATE)
    cc = cm.astype(f32).reshape(bsz, nc, CHUNK, SSM_GROUPS, SSM_STATE)
    da = (dt * a).reshape(bsz, nc, CHUNK, SSM_GROUPS, hg).transpose(0, 1, 3, 4, 2)
    a_cs = jnp.cumsum(da, axis=-1)
    causal = jnp.tril(jnp.ones((CHUNK, CHUNK), dtype=bool))
    seg = a_cs[..., :, None] - a_cs[..., None, :]
    decay_in = jnp.exp(jnp.where(causal, seg, -jnp.inf))
    cb = jnp.einsum("bclgn,bcsgn->bcgls", cc, bc)
    w_intra = cb[:, :, :, None] * decay_in
    y_diag = jnp.einsum("bcghls,bcsghp->bclghp", w_intra, xdt)
    decay_st = jnp.exp(a_cs[..., -1:] - a_cs).transpose(0, 1, 4, 2, 3)[..., None]
    states = jnp.einsum("bcsgn,bcsghp->bcghpn", bc, xdt * decay_st)
    chunk_decay = jnp.exp(a_cs[..., -1])

    def step(h, inp):
        st, dec = inp
        return h * dec[..., None, None] + st, h

    h0 = jnp.zeros((bsz, SSM_GROUPS, hg, SSM_HEAD_DIM, SSM_STATE), f32)
    _, h_prev = lax.scan(step, h0, (states.transpose(1, 0, 2, 3, 4, 5),
                                    chunk_decay.transpose(1, 0, 2, 3)))
    h_prev = h_prev.transpose(1, 0, 2, 3, 4, 5)
    decay_out = jnp.exp(a_cs).transpose(0, 1, 4, 2, 3)[..., None]
    y_off = jnp.einsum("bclgn,bcghpn->bclghp", cc, h_prev) * decay_out
    return (y_diag + y_off).reshape(bsz, lp, SSM_HEADS, SSM_HEAD_DIM)


def swa_sink_attention(q, k, v, sinks):
    f32 = jnp.float32
    bsz, seqlen = q.shape[:2]
    pad = BLOCK - N_META
    rep = ATTN_HEADS // ATTN_KV_HEADS
    k_meta, v_meta = k[:, :N_META], v[:, :N_META]
    padw = ((0, 0), (pad, 0), (0, 0), (0, 0))
    lp = seqlen + pad
    nb = lp // BLOCK
    qb = jnp.pad(q, padw).reshape(bsz, nb, BLOCK, ATTN_KV_HEADS, rep, HEAD_DIM)
    kb = jnp.pad(k, padw).reshape(bsz, nb, BLOCK, ATTN_KV_HEADS, HEAD_DIM)
    vb = jnp.pad(v, padw).reshape(bsz, nb, BLOCK, ATTN_KV_HEADS, HEAD_DIM)

    def band(t):
        prev = jnp.pad(t[:, :-1], ((0, 0), (1, 0), (0, 0), (0, 0), (0, 0)))
        return jnp.concatenate([prev, t], axis=2)

    kband, vband = band(kb), band(vb)
    qpos = jnp.arange(nb)[:, None] * BLOCK + jnp.arange(BLOCK)[None, :] - pad
    kpos = jnp.arange(nb)[:, None] * BLOCK + jnp.arange(-BLOCK, BLOCK)[None, :] - pad
    band_mask = ((kpos[:, None, :] <= qpos[:, :, None])
                 & (qpos[:, :, None] - kpos[:, None, :] < WINDOW)
                 & (kpos[:, None, :] >= N_META))
    meta_mask = jnp.arange(N_META)[None, None, :] <= qpos[:, :, None]
    mask = jnp.concatenate([meta_mask, band_mask], axis=-1)
    scale = HEAD_DIM ** -0.5
    s_meta = jnp.einsum("bnqkrd,bmkd->bnkrqm", qb, k_meta)
    s_band = jnp.einsum("bnqkrd,bnskd->bnkrqs", qb, kband)
    s = jnp.concatenate([s_meta, s_band], axis=-1).astype(f32) * scale
    s = jnp.where(mask[None, :, None, None], s, -jnp.inf)
    sink = jnp.broadcast_to(sinks.astype(f32).reshape(ATTN_KV_HEADS, rep)[None, None, :, :, None, None],
                            s.shape[:-1] + (1,))
    p = jax.nn.softmax(jnp.concatenate([s, sink], axis=-1), axis=-1)[..., :-1].astype(v.dtype)
    o = (jnp.einsum("bnkrqm,bmkd->bnqkrd", p[..., :N_META], v_meta)
         + jnp.einsum("bnkrqs,bnskd->bnqkrd", p[..., N_META:], vband))
    return o.reshape(bsz, lp, ATTN_WIDTH)[:, pad:]


def hybrid_mixer(h, w_in, conv_w, conv_b, dt_bias, a_log, d_skip, ssm_norm, attn_norm, sinks, w_out):
    f32 = jnp.float32
    bsz, seqlen, _ = h.shape
    proj = h @ w_in
    cuts = np.cumsum([ATTN_WIDTH, KV_WIDTH, KV_WIDTH, SSM_D_INNER, CONV_DIM]).tolist()
    q, k, v, z, xbc, dt_raw = jnp.split(proj, cuts, axis=-1)
    q = q.reshape(bsz, seqlen, ATTN_HEADS, HEAD_DIM)
    k = k.reshape(bsz, seqlen, ATTN_KV_HEADS, HEAD_DIM)
    v = v.reshape(bsz, seqlen, ATTN_KV_HEADS, HEAD_DIM)
    y_attn = rmsnorm(swa_sink_attention(q, k, v, sinks), attn_norm)
    xbc = jax.nn.silu(causal_dwconv(xbc, conv_w, conv_b))
    xs, bm, cm = jnp.split(xbc, [SSM_D_INNER, SSM_D_INNER + SSM_GROUPS * SSM_STATE], axis=-1)
    xs = xs.reshape(bsz, seqlen, SSM_HEADS, SSM_HEAD_DIM)
    bm = bm.reshape(bsz, seqlen, SSM_GROUPS, SSM_STATE)
    cm = cm.reshape(bsz, seqlen, SSM_GROUPS, SSM_STATE)
    dt = jax.nn.softplus(dt_raw.astype(f32) + dt_bias.astype(f32))
    a = -jnp.exp(a_log.astype(f32))
    pad = CHUNK - N_META

    def padl(t):
        return jnp.pad(t, ((0, 0), (pad, 0)) + ((0, 0),) * (t.ndim - 2))

    y = ssd_scan(padl(xs), padl(dt), a, padl(bm), padl(cm))[:, pad:]
    y = y + xs.astype(f32) * d_skip.astype(f32)[:, None]
    y = y.reshape(bsz, seqlen, SSM_D_INNER) * jax.nn.silu(z.astype(f32))
    yg = y.reshape(bsz, seqlen, SSM_GROUPS, SSM_D_INNER // SSM_GROUPS)
    yg = yg * lax.rsqrt(jnp.mean(yg * yg, axis=-1, keepdims=True) + EPS)
    y_ssm = (yg.reshape(bsz, seqlen, SSM_D_INNER) * ssm_norm.astype(f32)).astype(h.dtype)
    return jnp.concatenate([y_attn, y_ssm], axis=-1) @ w_out


def swiglu(t, w1, w3, w2):
    return (jax.nn.silu(t @ w1) * (t @ w3)) @ w2


def moe_swiglu(u, router, w1, w3, w2):
    t = u.reshape(-1, D_MODEL)
    logits = (t @ router).astype(jnp.float32)
    top_v, top_i = lax.top_k(logits, TOP_K)
    gates = jax.nn.softmax(top_v, axis=-1)
    comb = jnp.einsum("nk,nke->ne", gates,
                      jax.nn.one_hot(top_i, N_EXPERTS, dtype=jnp.float32)).astype(u.dtype)
    out = jnp.zeros_like(t)
    for e in range(N_EXPERTS):
        out = out + comb[:, e:e + 1] * swiglu(t, w1[e], w3[e], w2[e])
    return out.reshape(u.shape)


def setup_inputs(seed: int = 0) -> dict:
    key = jax.random.key(seed)
    ks = jax.random.split(key, 24)
    f = jnp.float32

    def nrm(k, shape, scale):
        return jax.random.normal(k, shape, f) * scale

    dt0 = jnp.exp(jax.random.uniform(ks[6], (DEPTH, SSM_HEADS), f)
                  * (jnp.log(0.1) - jnp.log(0.001)) + jnp.log(0.001))
    return {
        "x": nrm(ks[0], (BATCH, SEQ, D_MODEL), 1.0),
        "meta_tokens": nrm(ks[1], (N_META, D_MODEL), 1.0),
        "norm_mix": 1.0 + nrm(ks[2], (DEPTH, D_MODEL), 0.02),
        "w_in": nrm(ks[3], (DEPTH, D_MODEL, IN_PROJ), D_MODEL ** -0.5),
        "conv_w": nrm(ks[4], (DEPTH, CONV_DIM, CONV_K), CONV_K ** -0.5),
        "conv_b": nrm(ks[5], (DEPTH, CONV_DIM), 0.01),
        "dt_bias": dt0 + jnp.log(-jnp.expm1(-dt0)),
        "a_log": jnp.log(jax.random.uniform(ks[7], (DEPTH, SSM_HEADS), f, 1.0, 16.0)),
        "d_skip": 1.0 + nrm(ks[8], (DEPTH, SSM_HEADS), 0.1),
        "ssm_norm": 1.0 + nrm(ks[9], (DEPTH, SSM_D_INNER), 0.02),
        "attn_norm": 1.0 + nrm(ks[10], (DEPTH, ATTN_WIDTH), 0.02),
        "sinks": nrm(ks[11], (DEPTH, ATTN_HEADS), 0.5),
        "w_out": nrm(ks[12], (DEPTH, MIX_WIDTH, D_MODEL), MIX_WIDTH ** -0.5),
        "norm_ffn": 1.0 + nrm(ks[13], (DEPTH, D_MODEL), 0.02),
        "ffn_w1": nrm(ks[14], (N_DENSE, D_MODEL, D_FF), D_MODEL ** -0.5),
        "ffn_w3": nrm(ks[15], (N_DENSE, D_MODEL, D_FF), D_MODEL ** -0.5),
        "ffn_w2": nrm(ks[16], (N_DENSE, D_FF, D_MODEL), D_FF ** -0.5),
        "router": nrm(ks[17], (N_MOE, D_MODEL, N_EXPERTS), D_MODEL ** -0.5),
        "moe_w1": nrm(ks[18], (N_MOE, N_EXPERTS, D_MODEL, D_FF), D_MODEL ** -0.5),
        "moe_w3": nrm(ks[19], (N_MOE, N_EXPERTS, D_MODEL, D_FF), D_MODEL ** -0.5),
        "moe_w2": nrm(ks[20], (N_MOE, N_EXPERTS, D_FF, D_MODEL), D_FF ** -0.5),
        "final_norm": 1.0 + nrm(ks[21], (D_MODEL,), 0.02),
    }


def reference(x, meta_tokens, norm_mix, w_in, conv_w, conv_b, dt_bias, a_log, d_skip, ssm_norm,
              attn_norm, sinks, w_out, norm_ffn, ffn_w1, ffn_w3, ffn_w2, router, moe_w1, moe_w3,
              moe_w2, final_norm):
    bsz = x.shape[0]
    meta = jnp.broadcast_to(meta_tokens[None].astype(x.dtype), (bsz, N_META, D_MODEL))
    h = jnp.concatenate([meta, x], axis=1)
    for i in range(DEPTH):
        h = h + hybrid_mixer(rmsnorm(h, norm_mix[i]), w_in[i], conv_w[i], conv_b[i], dt_bias[i],
                             a_log[i], d_skip[i], ssm_norm[i], attn_norm[i], sinks[i], w_out[i])
        u = rmsnorm(h, norm_ffn[i])
        j = i // 2
        if i % 2 == 0:
            h = h + swiglu(u, ffn_w1[j], ffn_w3[j], ffn_w2[j])
        else:
            h = h + moe_swiglu(u, router[j], moe_w1[j], moe_w3[j], moe_w2[j])
    return rmsnorm(h, final_norm)[:, N_META:]
```

```python
import functools

import jax
import jax.numpy as jnp
from jax import lax
from jax.experimental import pallas as pl
from jax.experimental.pallas import tpu as pltpu

F32 = jnp.float32
BF16 = jnp.bfloat16

D_MODEL = 1024
N_META = 16
BLK = 128
PAD = BLK - N_META
ATTN_HEADS = 8
KV_HEADS = 2
HEAD_DIM = 64
ATTN_W = ATTN_HEADS * HEAD_DIM
KV_W = KV_HEADS * HEAD_DIM
SSM_HEADS = 24
SSM_P = 64
SSM_DI = SSM_HEADS * SSM_P
SSM_G = 2
SSM_N = 128
SSM_GW = SSM_DI // SSM_G
CONV_K = 4
CONV_DIM = SSM_DI + 2 * SSM_G * SSM_N
N_EXP = 8
EPS = 1e-6
NEG = -1e30

OFF_Q = 0
OFF_KV = ATTN_W
OFF_Z = ATTN_W + 2 * KV_W
OFF_XBC = OFF_Z + SSM_DI
OFF_DT = OFF_XBC + CONV_DIM

VMEM_LIMIT = 48 * 1024 * 1024


def _pick_tile(n, pref):
    t = pref
    while n % t:
        t //= 2
    return t


def _cparams(sem):
    return pltpu.CompilerParams(dimension_semantics=sem, vmem_limit_bytes=VMEM_LIMIT)


def _rms(xf, g):
    return xf * lax.rsqrt(jnp.mean(xf * xf, axis=-1, keepdims=True) + EPS) * g


def _sigmoid(x):
    return 1.0 / (1.0 + jnp.exp(-x))


def _split3(v):
    hi = v.astype(BF16)
    r1 = v - hi.astype(F32)
    mid = r1.astype(BF16)
    lo = (r1 - mid.astype(F32)).astype(BF16)
    return hi, mid, lo


def _norm_kernel(h_ref, g_ref, o_ref):
    o_ref[...] = _rms(h_ref[...], g_ref[...]).astype(o_ref.dtype)


def _norm_call(h, g, out_dtype=None):
    out_dtype = BF16 if out_dtype is None else out_dtype
    t = h.shape[0]
    tm = _pick_tile(t, 512)
    return pl.pallas_call(
        _norm_kernel,
        out_shape=jax.ShapeDtypeStruct((t, D_MODEL), out_dtype),
        grid=(t // tm,),
        in_specs=[pl.BlockSpec((tm, D_MODEL), lambda i: (i, 0)),
                  pl.BlockSpec((1, D_MODEL), lambda i: (0, 0))],
        out_specs=pl.BlockSpec((tm, D_MODEL), lambda i: (i, 0)),
        compiler_params=_cparams(("arbitrary",)),
    )(h, g)


def _inproj_kernel(x_ref, w_ref, wdt_ref, q_ref, kv_ref, z_ref, xbc_ref, dt_ref):
    x = x_ref[...]

    def mm(lo, n):
        return jnp.dot(x, w_ref[:, lo:lo + n], preferred_element_type=F32)

    q_ref[...] = (mm(OFF_Q, ATTN_W) * (HEAD_DIM ** -0.5)).astype(BF16)
    kv_ref[...] = mm(OFF_KV, 2 * KV_W).astype(BF16)
    for c in range(SSM_DI // 512):
        z_ref[:, 512 * c:512 * (c + 1)] = mm(OFF_Z + 512 * c, 512).astype(BF16)
    for c in range(CONV_DIM // 512):
        xbc_ref[:, 512 * c:512 * (c + 1)] = mm(OFF_XBC + 512 * c, 512).astype(BF16)
    dt_ref[...] = jnp.dot(x, wdt_ref[...], preferred_element_type=F32)


def _inproj_call(xn, w_main, w_dt):
    t = xn.shape[0]
    tm = _pick_tile(t, 512)
    row = lambda i: (i, 0)
    const = lambda i: (0, 0)
    return pl.pallas_call(
        _inproj_kernel,
        out_shape=(jax.ShapeDtypeStruct((t, ATTN_W), BF16),
                   jax.ShapeDtypeStruct((t, 2 * KV_W), BF16),
                   jax.ShapeDtypeStruct((t, SSM_DI), BF16),
                   jax.ShapeDtypeStruct((t, CONV_DIM), BF16),
                   jax.ShapeDtypeStruct((t, BLK), F32)),
        grid=(t // tm,),
        in_specs=[pl.BlockSpec((tm, D_MODEL), row),
                  pl.BlockSpec((D_MODEL, OFF_DT), const),
                  pl.BlockSpec((D_MODEL, BLK), const)],
        out_specs=(pl.BlockSpec((tm, ATTN_W), row),
                   pl.BlockSpec((tm, 2 * KV_W), row),
                   pl.BlockSpec((tm, SSM_DI), row),
                   pl.BlockSpec((tm, CONV_DIM), row),
                   pl.BlockSpec((tm, BLK), row)),
        compiler_params=_cparams(("arbitrary",)),
    )(xn, w_main, w_dt)


def _attn_kernel(sink_ref, q_ref, kvc_ref, kvp_ref, kvm_ref, g_ref, o_ref):
    n = pl.program_id(1)
    row = lax.broadcasted_iota(jnp.int32, (BLK, BLK), 0)
    col = lax.broadcasted_iota(jnp.int32, (BLK, BLK), 1)
    qpos = n * BLK + row - PAD
    vis_meta = jnp.logical_and(col >= PAD, col - PAD <= qpos)
    vis_prev = col > row + jnp.where(n >= 2, 0, BLK)
    vis_cur = col <= row - jnp.where(n >= 1, 0, BLK)
    nt = (((1,), (1,)), ((), ()))
    rep = ATTN_HEADS // KV_HEADS
    outs = []
    for h in range(ATTN_HEADS):
        g = h // rep
        qh = q_ref[:, HEAD_DIM * h:HEAD_DIM * (h + 1)]
        ks = slice(HEAD_DIM * g, HEAD_DIM * (g + 1))
        vs = slice(KV_W + HEAD_DIM * g, KV_W + HEAD_DIM * (g + 1))

        def score(kv_ref, vis):
            s = lax.dot_general(qh, kv_ref[:, ks], nt, preferred_element_type=F32)
            return jnp.where(vis, s, NEG)

        s_m = score(kvm_ref, vis_meta)
        s_p = score(kvp_ref, vis_prev)
        s_c = score(kvc_ref, vis_cur)
        sink = sink_ref[h]
        m = jnp.maximum(jnp.maximum(jnp.max(s_m, axis=-1, keepdims=True),
                                    jnp.max(s_p, axis=-1, keepdims=True)),
                        jnp.maximum(jnp.max(s_c, axis=-1, keepdims=True), sink))
        p_m = jnp.exp(s_m - m)
        p_p = jnp.exp(s_p - m)
        p_c = jnp.exp(s_c - m)
        denom = (jnp.sum(p_m, axis=-1, keepdims=True) + jnp.sum(p_p, axis=-1, keepdims=True)
                 + jnp.sum(p_c, axis=-1, keepdims=True) + jnp.exp(sink - m))
        o = (jnp.dot(p_m.astype(BF16), kvm_ref[:, vs], preferred_element_type=F32)
             + jnp.dot(p_p.astype(BF16), kvp_ref[:, vs], preferred_element_type=F32)
             + jnp.dot(p_c.astype(BF16), kvc_ref[:, vs], preferred_element_type=F32))
        outs.append(o / denom)
    y = jnp.concatenate(outs, axis=-1)
    o_ref[...] = _rms(y, g_ref[...]).astype(o_ref.dtype)


def _attn_call(q, kv, sinks, gain, bsz, nb):
    t = q.shape[0]
    grid_spec = pltpu.PrefetchScalarGridSpec(
        num_scalar_prefetch=0,
        grid=(bsz, nb),
        in_specs=[pl.BlockSpec(memory_space=pltpu.SMEM),
                  pl.BlockSpec((BLK, ATTN_W), lambda b, n: (b * nb + n, 0)),
                  pl.BlockSpec((BLK, 2 * KV_W), lambda b, n: (b * nb + n, 0)),
                  pl.BlockSpec((BLK, 2 * KV_W), lambda b, n: (b * nb + jnp.maximum(n - 1, 0), 0)),
                  pl.BlockSpec((BLK, 2 * KV_W), lambda b, n: (b * nb, 0)),
                  pl.BlockSpec((1, ATTN_W), lambda b, n: (0, 0))],
        out_specs=pl.BlockSpec((BLK, ATTN_W), lambda b, n: (b * nb + n, 0)),
    )
    return pl.pallas_call(
        _attn_kernel,
        out_shape=jax.ShapeDtypeStruct((t, ATTN_W), BF16),
        grid_spec=grid_spec,
        compiler_params=_cparams(("arbitrary", "arbitrary")),
    )(sinks, q, kv, kv, kv, gain)


def _ssd_kernel(xbc_ref, z_ref, dt_ref, cw_ref, cb_ref, dtb_ref, alog_ref, dsk_ref, nw_ref,
                y_ref, prev_ref, state_ref, xd_ref, yd_ref, dout_ref):
    c = pl.program_id(1)

    @pl.when(c == 0)
    def _():
        prev_ref[...] = jnp.zeros_like(prev_ref)
        state_ref[...] = jnp.zeros_like(state_ref)

    valid = jnp.logical_or(c > 0, lax.broadcasted_iota(jnp.int32, (BLK, 1), 0) >= PAD)
    x = jnp.where(valid, xbc_ref[...].astype(F32), 0.0)
    prev = prev_ref[...]
    rows = lax.broadcasted_iota(jnp.int32, (BLK, CONV_DIM), 0)
    acc = x * cw_ref[CONV_K - 1:CONV_K, :] + cb_ref[...]
    for s in range(1, CONV_K):
        shifted = jnp.where(rows < s, pltpu.roll(prev, s, 0), pltpu.roll(x, s, 0))
        acc = acc + shifted * cw_ref[CONV_K - 1 - s:CONV_K - s, :]
    prev_ref[...] = x
    act = acc * _sigmoid(acc)
    bm = act[:, SSM_DI:SSM_DI + SSM_G * SSM_N]
    cm = act[:, SSM_DI + SSM_G * SSM_N:]

    dtr = dt_ref[...] + dtb_ref[...]
    dt = jnp.maximum(dtr, 0.0) + jnp.log1p(jnp.exp(-jnp.abs(dtr)))
    dt = jnp.where(valid, dt, 0.0)
    da = dt * (-jnp.exp(alog_ref[...]))

    row = lax.broadcasted_iota(jnp.int32, (BLK, BLK), 0)
    col = lax.broadcasted_iota(jnp.int32, (BLK, BLK), 1)
    causal = row >= col
    tril = jnp.where(causal, 1.0, 0.0).astype(BF16)
    a_cs = sum(jnp.dot(tril, part, preferred_element_type=F32) for part in _split3(da))
    a_cs_t = a_cs.T
    d_state = jnp.exp(a_cs[BLK - 1:BLK, :] - a_cs)
    d_out = jnp.exp(a_cs)
    dtd = dt * d_state

    nt = (((1,), (1,)), ((), ()))
    cbs = [lax.dot_general(cm[:, SSM_N * g:SSM_N * (g + 1)].astype(BF16),
                           bm[:, SSM_N * g:SSM_N * (g + 1)].astype(BF16),
                           nt, preferred_element_type=F32) for g in range(SSM_G)]
    lo_half = col < SSM_P
    pairs = SSM_HEADS // 2
    for j in range(pairs):
        g = j // (pairs // SSM_G)
        h0, h1 = 2 * j, 2 * j + 1
        sl = slice(BLK * j, BLK * (j + 1))

        def per_head(v):
            return jnp.where(lo_half, v[:, h0:h0 + 1], v[:, h1:h1 + 1])

        xs_p = act[:, sl]
        xdt = xs_p * per_head(dt)
        xd_ref[:, sl] = (xs_p * per_head(dtd)).astype(BF16)
        dout_ref[:, sl] = per_head(d_out)

        def intra(h):
            seg = a_cs[:, h:h + 1] - a_cs_t[h:h + 1, :]
            return (cbs[g] * jnp.exp(jnp.where(causal, seg, NEG))).astype(BF16)

        yd_ref[:, sl] = (
            jnp.dot(intra(h0), jnp.where(lo_half, xdt, 0.0).astype(BF16), preferred_element_type=F32)
            + jnp.dot(intra(h1), jnp.where(lo_half, 0.0, xdt).astype(BF16), preferred_element_type=F32))

    for g in range(SSM_G):
        gs = slice(SSM_GW * g, SSM_GW * (g + 1))
        st = state_ref[g]
        cm_g = cm[:, SSM_N * g:SSM_N * (g + 1)].astype(BF16)
        bm_t = bm[:, SSM_N * g:SSM_N * (g + 1)].T.astype(BF16)
        y_off = jnp.dot(cm_g, st.astype(BF16), preferred_element_type=F32) * dout_ref[:, gs]
        state_ref[g] = (st * dout_ref[BLK - 1:BLK, gs]
                        + jnp.dot(bm_t, xd_ref[:, gs], preferred_element_type=F32))
        y = yd_ref[:, gs] + y_off + act[:, gs] * dsk_ref[:, gs]
        zz = z_ref[:, gs].astype(F32)
        y = y * (zz * _sigmoid(zz))
        y_ref[:, gs] = _rms(y, nw_ref[:, gs]).astype(y_ref.dtype)


def _ssd_call(xbc, z, dt_raw, conv_w, conv_b, dt_bias, a_log, d_skip, norm_w, bsz, nb):
    t = xbc.shape[0]
    blk = lambda b, n: (b * nb + n, 0)
    const = lambda b, n: (0, 0)
    return pl.pallas_call(
        _ssd_kernel,
        out_shape=jax.ShapeDtypeStruct((t, SSM_DI), BF16),
        grid=(bsz, nb),
        in_specs=[pl.BlockSpec((BLK, CONV_DIM), blk),
                  pl.BlockSpec((BLK, SSM_DI), blk),
                  pl.BlockSpec((BLK, BLK), blk),
                  pl.BlockSpec((CONV_K, CONV_DIM), const),
                  pl.BlockSpec((1, CONV_DIM), const),
                  pl.BlockSpec((1, BLK), const),
                  pl.BlockSpec((1, BLK), const),
                  pl.BlockSpec((1, SSM_DI), const),
                  pl.BlockSpec((1, SSM_DI), const)],
        out_specs=pl.BlockSpec((BLK, SSM_DI), blk),
        scratch_shapes=[pltpu.VMEM((BLK, CONV_DIM), F32),
                        pltpu.VMEM((SSM_G, SSM_N, SSM_GW), F32),
                        pltpu.VMEM((BLK, SSM_DI), BF16),
                        pltpu.VMEM((BLK, SSM_DI), F32),
                        pltpu.VMEM((BLK, SSM_DI), F32)],
        compiler_params=_cparams(("arbitrary", "arbitrary")),
    )(xbc, z, dt_raw, conv_w, conv_b, dt_bias, a_log, d_skip, norm_w)


def _outproj_kernel(ya_ref, ys_ref, h_ref, wa_ref, ws_ref, g_ref, hn_ref, u_ref):
    hn = (h_ref[...]
          + jnp.dot(ya_ref[...], wa_ref[...], preferred_element_type=F32)
          + jnp.dot(ys_ref[...], ws_ref[...], preferred_element_type=F32))
    hn_ref[...] = hn
    u_ref[...] = _rms(hn, g_ref[...]).astype(u_ref.dtype)


def _outproj_call(ya, ys, h, wa, ws, g, u_dtype):
    t = h.shape[0]
    tm = _pick_tile(t, 512)
    row = lambda i: (i, 0)
    const = lambda i: (0, 0)
    return pl.pallas_call(
        _outproj_kernel,
        out_shape=(jax.ShapeDtypeStruct((t, D_MODEL), F32),
                   jax.ShapeDtypeStruct((t, D_MODEL), u_dtype)),
        grid=(t // tm,),
        in_specs=[pl.BlockSpec((tm, ATTN_W), row),
                  pl.BlockSpec((tm, SSM_DI), row),
                  pl.BlockSpec((tm, D_MODEL), row),
                  pl.BlockSpec((ATTN_W, D_MODEL), const),
                  pl.BlockSpec((SSM_DI, D_MODEL), const),
                  pl.BlockSpec((1, D_MODEL), const)],
        out_specs=(pl.BlockSpec((tm, D_MODEL), row),
                   pl.BlockSpec((tm, D_MODEL), row)),
        compiler_params=_cparams(("arbitrary",)),
    )(ya, ys, h, wa, ws, g)


def _ffn_kernel(u_ref, h_ref, w1_ref, w3_ref, w2_ref, g_ref, hn_ref, xn_ref, acc_ref):
    f = pl.program_id(1)

    @pl.when(f == 0)
    def _():
        acc_ref[...] = jnp.zeros_like(acc_ref)

    u = u_ref[...]
    a = jnp.dot(u, w1_ref[...], preferred_element_type=F32)
    b = jnp.dot(u, w3_ref[...], preferred_element_type=F32)
    act = (a * _sigmoid(a) * b).astype(BF16)
    acc_ref[...] += jnp.dot(act, w2_ref[...], preferred_element_type=F32)

    @pl.when(f == pl.num_programs(1) - 1)
    def _():
        hn = h_ref[...] + acc_ref[...]
        hn_ref[...] = hn
        xn_ref[...] = _rms(hn, g_ref[...]).astype(xn_ref.dtype)


def _ffn_call(u, h, w1, w3, w2, g):
    t = h.shape[0]
    d_ff = w1.shape[1]
    tm = _pick_tile(t, 1024)
    tf = _pick_tile(d_ff, 512)
    return pl.pallas_call(
        _ffn_kernel,
        out_shape=(jax.ShapeDtypeStruct((t, D_MODEL), F32),
                   jax.ShapeDtypeStruct((t, D_MODEL), BF16)),
        grid=(t // tm, d_ff // tf),
        in_specs=[pl.BlockSpec((tm, D_MODEL), lambda i, f: (i, 0)),
                  pl.BlockSpec((tm, D_MODEL), lambda i, f: (i, 0)),
                  pl.BlockSpec((D_MODEL, tf), lambda i, f: (0, f)),
                  pl.BlockSpec((D_MODEL, tf), lambda i, f: (0, f)),
                  pl.BlockSpec((tf, D_MODEL), lambda i, f: (f, 0)),
                  pl.BlockSpec((1, D_MODEL), lambda i, f: (0, 0))],
        out_specs=(pl.BlockSpec((tm, D_MODEL), lambda i, f: (i, 0)),
                   pl.BlockSpec((tm, D_MODEL), lambda i, f: (i, 0))),
        scratch_shapes=[pltpu.VMEM((tm, D_MODEL), F32)],
        compiler_params=_cparams(("arbitrary", "arbitrary")),
    )(u, h, w1, w3, w2, g)


R_E1, R_E2, R_G1, R_G2, R_K1, R_K2 = range(6)


def _router_kernel(h_ref, g_ref, rhi_ref, rlo_ref, route_ref, cnt_ref):
    b = pl.program_id(0)
    n = pl.program_id(1)

    @pl.when(jnp.logical_and(b == 0, n == 0))
    def _():
        cnt_ref[...] = jnp.zeros_like(cnt_ref)

    u = _rms(h_ref[...], g_ref[...])
    u_hi = u.astype(BF16)
    u_lo = (u - u_hi.astype(F32)).astype(BF16)
    logits = (jnp.dot(u_hi, rhi_ref[...], preferred_element_type=F32)
              + jnp.dot(u_lo, rhi_ref[...], preferred_element_type=F32)
              + jnp.dot(u_hi, rlo_ref[...], preferred_element_type=F32))
    lane = lax.broadcasted_iota(jnp.int32, (BLK, BLK), 1)
    logits = jnp.where(lane < N_EXP, logits, -jnp.inf)
    m1 = jnp.max(logits, axis=-1, keepdims=True)
    i1 = jnp.min(jnp.where(logits == m1, lane, BLK), axis=-1, keepdims=True)
    rest = jnp.where(lane == i1, -jnp.inf, logits)
    m2 = jnp.max(rest, axis=-1, keepdims=True)
    i2 = jnp.min(jnp.where(rest == m2, lane, BLK), axis=-1, keepdims=True)
    e21 = jnp.exp(m2 - m1)
    g1 = 1.0 / (1.0 + e21)
    g2 = e21 / (1.0 + e21)

    valid = jnp.logical_or(n > 0, lax.broadcasted_iota(jnp.int32, (BLK, 1), 0) >= PAD)
    sel1 = jnp.logical_and(lane == i1, valid)
    sel2 = jnp.logical_and(lane == i2, valid)
    onehot = jnp.where(sel1, 1.0, 0.0) + jnp.where(sel2, 1.0, 0.0)
    row = lax.broadcasted_iota(jnp.int32, (BLK, BLK), 0)
    strict = jnp.where(row > lane, 1.0, 0.0).astype(BF16)
    before = jnp.dot(strict, onehot.astype(BF16), preferred_element_type=F32) + cnt_ref[...]
    k1 = jnp.sum(jnp.where(sel1, before, 0.0), axis=-1, keepdims=True)
    k2 = jnp.sum(jnp.where(sel2, before, 0.0), axis=-1, keepdims=True)
    cnt_ref[...] += jnp.sum(onehot, axis=0, keepdims=True)

    rec = jnp.zeros((BLK, BLK), F32)
    for idx, val in ((R_E1, i1.astype(F32)), (R_E2, i2.astype(F32)),
                     (R_G1, jnp.where(valid, g1, 0.0)), (R_G2, jnp.where(valid, g2, 0.0)),
                     (R_K1, k1), (R_K2, k2)):
        rec = jnp.where(lane == idx, val, rec)
    route_ref[...] = rec


def _router_call(h, g, r_hi, r_lo, bsz, nb):
    t = h.shape[0]
    blk = lambda b, n: (b * nb + n, 0)
    const = lambda b, n: (0, 0)
    return pl.pallas_call(
        _router_kernel,
        out_shape=(jax.ShapeDtypeStruct((t, BLK), F32),
                   jax.ShapeDtypeStruct((1, BLK), F32)),
        grid=(bsz, nb),
        in_specs=[pl.BlockSpec((BLK, D_MODEL), blk),
                  pl.BlockSpec((1, D_MODEL), const),
                  pl.BlockSpec((D_MODEL, BLK), const),
                  pl.BlockSpec((D_MODEL, BLK), const)],
        out_specs=(pl.BlockSpec((BLK, BLK), blk),
                   pl.BlockSpec((1, BLK), const)),
        compiler_params=_cparams(("arbitrary", "arbitrary")),
    )(h, g, r_hi, r_lo)


def _dispatch_kernel(pos_ref, u_hbm, xs_in, xs_hbm, sem, *, tm):
    del xs_in
    base = pl.program_id(0) * tm

    def row_copy(t, p):
        return pltpu.make_async_copy(u_hbm.at[pl.ds(base + t, 1)], xs_hbm.at[pl.ds(p, 1)], sem)

    def issue(t, carry):
        for k in range(2):
            p = pos_ref[2 * t + k]

            @pl.when(p >= 0)
            def _():
                row_copy(t, p).start()
        return carry

    def drain(t, carry):
        for k in range(2):
            p = pos_ref[2 * t + k]

            @pl.when(p >= 0)
            def _():
                row_copy(t, p).wait()
        return carry

    lax.fori_loop(0, tm, issue, 0)
    lax.fori_loop(0, tm, drain, 0)


def _dispatch_call(pos_flat, u, r_max):
    t = u.shape[0]
    tm = _pick_tile(t, 512)
    xs0 = jnp.zeros((r_max, D_MODEL), u.dtype)
    return pl.pallas_call(
        functools.partial(_dispatch_kernel, tm=tm),
        out_shape=jax.ShapeDtypeStruct((r_max, D_MODEL), u.dtype),
        grid=(t // tm,),
        in_specs=[pl.BlockSpec((2 * tm,), lambda i: (i,), memory_space=pltpu.SMEM),
                  pl.BlockSpec(memory_space=pl.ANY),
                  pl.BlockSpec(memory_space=pl.ANY)],
        out_specs=pl.BlockSpec(memory_space=pl.ANY),
        scratch_shapes=[pltpu.SemaphoreType.DMA(())],
        input_output_aliases={2: 0},
        compiler_params=pltpu.CompilerParams(dimension_semantics=("arbitrary",),
                                             has_side_effects=True),
    )(pos_flat, u, xs0)


def _moe_kernel(te_ref, nv_ref, x_ref, w1_ref, w3_ref, w2_ref, y_ref, acc_ref, xb_ref):
    del te_ref
    i = pl.program_id(0)
    f = pl.program_id(1)

    @pl.when(i < nv_ref[0])
    def _():
        @pl.when(f == 0)
        def _():
            acc_ref[...] = jnp.zeros_like(acc_ref)
            xb_ref[...] = x_ref[...].astype(BF16)

        x = xb_ref[...]
        a = jnp.dot(x, w1_ref[...], preferred_element_type=F32)
        b = jnp.dot(x, w3_ref[...], preferred_element_type=F32)
        act = (a * _sigmoid(a) * b).astype(BF16)
        acc_ref[...] += jnp.dot(act, w2_ref[...], preferred_element_type=F32)

        @pl.when(f == pl.num_programs(1) - 1)
        def _():
            y_ref[...] = acc_ref[...]

    @pl.when(jnp.logical_and(i >= nv_ref[0], f == 0))
    def _():
        y_ref[...] = jnp.zeros_like(y_ref)


def _moe_call(tile_expert, n_valid, xs, w1, w3, w2, tm):
    r_max = xs.shape[0]
    d_ff = w1.shape[2]
    tf = _pick_tile(d_ff, 512)
    nf = d_ff // tf

    def f_eff(i, f, nv):
        return jnp.where(i < nv[0], f, nf - 1)

    grid_spec = pltpu.PrefetchScalarGridSpec(
        num_scalar_prefetch=2,
        grid=(r_max // tm, nf),
        in_specs=[pl.BlockSpec((tm, D_MODEL), lambda i, f, te, nv: (i, 0)),
                  pl.BlockSpec((None, D_MODEL, tf), lambda i, f, te, nv: (te[i], 0, f_eff(i, f, nv))),
                  pl.BlockSpec((None, D_MODEL, tf), lambda i, f, te, nv: (te[i], 0, f_eff(i, f, nv))),
                  pl.BlockSpec((None, tf, D_MODEL), lambda i, f, te, nv: (te[i], f_eff(i, f, nv), 0))],
        out_specs=pl.BlockSpec((tm, D_MODEL), lambda i, f, te, nv: (i, 0)),
        scratch_shapes=[pltpu.VMEM((tm, D_MODEL), F32),
                        pltpu.VMEM((tm, D_MODEL), BF16)],
    )
    return pl.pallas_call(
        _moe_kernel,
        out_shape=jax.ShapeDtypeStruct((r_max, D_MODEL), F32),
        grid_spec=grid_spec,
        compiler_params=_cparams(("arbitrary", "arbitrary")),
    )(tile_expert, n_valid, xs, w1, w3, w2)


def _combine_kernel(pos_ref, h_ref, route_ref, g_ref, ys_hbm, *rest, tm, emit_h):
    if emit_h:
        hn_ref, xn_ref, y0_ref, y1_ref, sem = rest
    else:
        xn_ref, y0_ref, y1_ref, sem = rest
    bufs = (y0_ref, y1_ref)

    def row_copy(t, k):
        p = pos_ref[2 * t + k]
        return pltpu.make_async_copy(ys_hbm.at[pl.ds(p, 1)], bufs[k].at[pl.ds(t, 1)], sem)

    def issue(t, carry):
        for k in range(2):
            row_copy(t, k).start()
        return carry

    def drain(t, carry):
        for k in range(2):
            row_copy(t, k).wait()
        return carry

    lax.fori_loop(0, tm, issue, 0)
    lax.fori_loop(0, tm, drain, 0)
    hn = (h_ref[...]
          + route_ref[:, R_G1:R_G1 + 1] * y0_ref[...]
          + route_ref[:, R_G2:R_G2 + 1] * y1_ref[...])
    if emit_h:
        hn_ref[...] = hn
    xn_ref[...] = _rms(hn, g_ref[...]).astype(xn_ref.dtype)


def _combine_call(pos_flat, h, route, g, ys, emit_h):
    t = h.shape[0]
    tm = _pick_tile(t, 256)
    row = lambda i: (i, 0)
    if emit_h:
        out_shape = (jax.ShapeDtypeStruct((t, D_MODEL), F32), jax.ShapeDtypeStruct((t, D_MODEL), BF16))
        out_specs = (pl.BlockSpec((tm, D_MODEL), row), pl.BlockSpec((tm, D_MODEL), row))
    else:
        out_shape = jax.ShapeDtypeStruct((t, D_MODEL), F32)
        out_specs = pl.BlockSpec((tm, D_MODEL), row)
    return pl.pallas_call(
        functools.partial(_combine_kernel, tm=tm, emit_h=emit_h),
        out_shape=out_shape,
        grid=(t // tm,),
        in_specs=[pl.BlockSpec((2 * tm,), lambda i: (i,), memory_space=pltpu.SMEM),
                  pl.BlockSpec((tm, D_MODEL), row),
                  pl.BlockSpec((tm, BLK), row),
                  pl.BlockSpec((1, D_MODEL), lambda i: (0, 0)),
                  pl.BlockSpec(memory_space=pl.ANY)],
        out_specs=out_specs,
        scratch_shapes=[pltpu.VMEM((tm, D_MODEL), F32),
                        pltpu.VMEM((tm, D_MODEL), F32),
                        pltpu.SemaphoreType.DMA(())],
        compiler_params=_cparams(("arbitrary",)),
    )(pos_flat, h, route, g, ys)


def _moe_layer(h, u, g_ffn, router, w1, w3, w2, g_next, emit_h, bsz, nb):
    t = h.shape[0]
    tm = 512
    n_tok = bsz * (nb * BLK - PAD)
    r_max = -(-(2 * n_tok + N_EXP * (tm - 1)) // tm) * tm
    n_tiles = r_max // tm

    r_pad = jnp.zeros((D_MODEL, BLK), F32).at[:, :N_EXP].set(router)
    r_hi = r_pad.astype(BF16)
    r_lo = (r_pad - r_hi.astype(F32)).astype(BF16)
    route, counts = _router_call(h, g_ffn, r_hi, r_lo, bsz, nb)

    cnt = counts[0, :N_EXP].astype(jnp.int32)
    padded = (cnt + tm - 1) // tm * tm
    ends = jnp.cumsum(padded)
    starts = ends - padded
    experts = route[:, R_E1:R_E2 + 1].astype(jnp.int32)
    ranks = route[:, R_K1:R_K2 + 1].astype(jnp.int32)
    slot = starts[experts] + ranks
    valid = (jnp.arange(t, dtype=jnp.int32) % (nb * BLK) >= PAD)[:, None]
    pos_scatter = jnp.where(valid, slot, -1).reshape(-1)
    pos_gather = jnp.where(valid, slot, 0).reshape(-1)
    n_valid = (ends[N_EXP - 1] // tm).astype(jnp.int32).reshape(1)
    tile_start = jnp.arange(n_tiles, dtype=jnp.int32) * tm
    tile_start = jnp.minimum(tile_start, (n_valid[0] - 1) * tm)
    tile_expert = jnp.sum(ends[None, :] <= tile_start[:, None], axis=1).astype(jnp.int32)

    xs = _dispatch_call(pos_scatter, u, r_max)
    ys = _moe_call(tile_expert, n_valid, xs, w1, w3, w2, tm)
    return _combine_call(pos_gather, h, route, g_next, ys, emit_h)


def kernel(x, meta_tokens, norm_mix, w_in, conv_w, conv_b, dt_bias, a_log, d_skip, ssm_norm,
           attn_norm, sinks, w_out, norm_ffn, ffn_w1, ffn_w3, ffn_w2, router, moe_w1, moe_w3,
           moe_w2, final_norm):
    bsz, seq, _ = x.shape
    depth = w_in.shape[0]
    assert seq % BLK == 0
    nb = seq // BLK + 1
    lp = nb * BLK

    meta = jnp.broadcast_to(meta_tokens[None].astype(F32), (bsz, N_META, D_MODEL))
    h = jnp.concatenate([jnp.zeros((bsz, PAD, D_MODEL), F32), meta, x], axis=1).reshape(bsz * lp, D_MODEL)

    def row(v, width=None):
        v = v.astype(F32).reshape(1, -1)
        if width is not None:
            v = jnp.pad(v, ((0, 0), (0, width - v.shape[1])))
        return v

    xn = _norm_call(h, row(norm_mix[0]))
    out = None
    for i in range(depth):
        w_main = w_in[i, :, :OFF_DT].astype(BF16)
        w_dt = jnp.pad(w_in[i, :, OFF_DT:], ((0, 0), (0, BLK - SSM_HEADS))).astype(BF16)
        q, kv, z, xbc, dt_raw = _inproj_call(xn, w_main, w_dt)
        y_attn = _attn_call(q, kv, sinks[i].astype(F32), row(attn_norm[i]), bsz, nb)
        y_ssm = _ssd_call(xbc, z, dt_raw, conv_w[i].T.astype(F32), row(conv_b[i]),
                          row(dt_bias[i], BLK), row(a_log[i], BLK),
                          row(jnp.repeat(d_skip[i], SSM_P)), row(ssm_norm[i]), bsz, nb)
        wo = w_out[i].astype(BF16)
        h, u = _outproj_call(y_attn, y_ssm, h, wo[:ATTN_W], wo[ATTN_W:], row(norm_ffn[i]),
                             BF16 if i % 2 == 0 else F32)
        last = i == depth - 1
        g_next = row(final_norm) if last else row(norm_mix[i + 1])
        j = i // 2
        if i % 2 == 0:
            h, xn = _ffn_call(u, h, ffn_w1[j].astype(BF16), ffn_w3[j].astype(BF16),
                              ffn_w2[j].astype(BF16), g_next)
            if last:
                out = _norm_call(h, g_next, F32)
        else:
            res = _moe_layer(h, u, row(norm_ffn[i]), router[j], moe_w1[j].astype(BF16),
                             moe_w3[j].astype(BF16), moe_w2[j].astype(BF16), g_next,
                             not last, bsz, nb)
            if last:
                out = res
            else:
                h, xn = res
    return out.reshape(bsz, lp, D_MODEL)[:, BLK:].astype(x.dtype)
```

```python
import functools

import numpy as np

import jax
import jax.numpy as jnp
from jax import lax
from jax.experimental import pallas as pl
from jax.experimental.pallas import tpu as pltpu

F32 = jnp.float32
BF16 = jnp.bfloat16

D_MODEL = 1024
N_META = 16
BLK = 128
SUBLANES = 8
PAD = BLK - N_META
ATTN_HEADS = 8
KV_HEADS = 2
HEAD_DIM = 64
ATTN_W = ATTN_HEADS * HEAD_DIM
KV_W = KV_HEADS * HEAD_DIM
SSM_HEADS = 24
SSM_P = 64
SSM_DI = SSM_HEADS * SSM_P
SSM_G = 2
SSM_N = 128
SSM_GW = SSM_DI // SSM_G
CONV_K = 4
CONV_DIM = SSM_DI + 2 * SSM_G * SSM_N
N_EXP = 8
EPS = 1e-6
NEG = -1e30

SRC_K = ATTN_W
SRC_V = ATTN_W + KV_W
SRC_Z = ATTN_W + 2 * KV_W
SRC_DT = SRC_Z + SSM_DI + CONV_DIM
KVX_W = 4 * KV_W
OFF_Q = 0
OFF_KVX = ATTN_W
OFF_Z = OFF_KVX + KVX_W
OFF_XBC = OFF_Z + SSM_DI
W_MAIN = OFF_XBC + CONV_DIM

MOE_TM = 512
DMA_UNROLL = 8
VMEM_LIMIT = 48 * 1024 * 1024


def _pick_tile(n, pref):
    t = pref
    while n % t:
        t //= 2
    return t


def _cparams(sem):
    return pltpu.CompilerParams(dimension_semantics=sem, vmem_limit_bytes=VMEM_LIMIT)


def _rms(xf, g):
    return xf * lax.rsqrt(jnp.mean(xf * xf, axis=-1, keepdims=True) + EPS) * g


def _sigmoid(x):
    return 1.0 / (1.0 + jnp.exp(-x))


def _split3(v):
    hi = v.astype(BF16)
    r1 = v - hi.astype(F32)
    mid = r1.astype(BF16)
    lo = (r1 - mid.astype(F32)).astype(BF16)
    return hi, mid, lo


def _norm_kernel(h_ref, g_ref, o_ref):
    o_ref[...] = _rms(h_ref[...], g_ref[...]).astype(o_ref.dtype)


def _norm_call(h, g, out_dtype=None):
    out_dtype = BF16 if out_dtype is None else out_dtype
    t = h.shape[0]
    tm = _pick_tile(t, 512)
    return pl.pallas_call(
        _norm_kernel,
        out_shape=jax.ShapeDtypeStruct((t, D_MODEL), out_dtype),
        grid=(t // tm,),
        in_specs=[pl.BlockSpec((tm, D_MODEL), lambda i: (i, 0)),
                  pl.BlockSpec((1, D_MODEL), lambda i: (0, 0))],
        out_specs=pl.BlockSpec((tm, D_MODEL), lambda i: (i, 0)),
        compiler_params=_cparams(("arbitrary",)),
        name="rmsnorm",
    )(h, g)


def _embed_kernel(x_ref, head_ref, g_ref, h_ref, xn_ref):
    h = jnp.where(pl.program_id(1) == 0, head_ref[...], x_ref[...].astype(F32))
    h_ref[...] = h
    xn_ref[...] = _rms(h, g_ref[...]).astype(xn_ref.dtype)


def _embed_call(x, meta_tokens, g):
    bsz, seq, _ = x.shape
    nb = seq // BLK + 1
    t = bsz * nb * BLK
    head = jnp.pad(meta_tokens.astype(F32), ((PAD, 0), (0, 0)))
    blk = lambda b, n: (b * nb + n, 0)
    const = lambda b, n: (0, 0)
    return pl.pallas_call(
        _embed_kernel,
        out_shape=(jax.ShapeDtypeStruct((t, D_MODEL), F32),
                   jax.ShapeDtypeStruct((t, D_MODEL), BF16)),
        grid=(bsz, nb),
        in_specs=[pl.BlockSpec((BLK, D_MODEL), lambda b, n: (b * (nb - 1) + jnp.maximum(n - 1, 0), 0)),
                  pl.BlockSpec((BLK, D_MODEL), const),
                  pl.BlockSpec((1, D_MODEL), const)],
        out_specs=(pl.BlockSpec((BLK, D_MODEL), blk),
                   pl.BlockSpec((BLK, D_MODEL), blk)),
        compiler_params=_cparams(("arbitrary", "arbitrary")),
        name="embed",
    )(x.reshape(bsz * seq, D_MODEL), head, g)


def _inproj_kernel(x_ref, w_ref, wdt_ref, q_ref, kvx_ref, z_ref, xbc_ref, dt_ref):
    x = x_ref[...]

    def mm(lo, n):
        return jnp.dot(x, w_ref[:, lo:lo + n], preferred_element_type=F32)

    q_ref[...] = (mm(OFF_Q, ATTN_W) * (HEAD_DIM ** -0.5)).astype(BF16)
    lane = lax.broadcasted_iota(jnp.int32, (1, KVX_W), 1)
    ones = jnp.where(jnp.logical_and(lane >= 2 * KV_W, (lane & HEAD_DIM) != 0), 1.0, 0.0)
    kvx_ref[...] = (mm(OFF_KVX, KVX_W) + ones).astype(BF16)
    for c in range(SSM_DI // 512):
        z_ref[:, 512 * c:512 * (c + 1)] = mm(OFF_Z + 512 * c, 512).astype(BF16)
    for c in range(CONV_DIM // 512):
        xbc_ref[:, 512 * c:512 * (c + 1)] = mm(OFF_XBC + 512 * c, 512).astype(BF16)
    dt_ref[...] = jnp.dot(x, wdt_ref[...], preferred_element_type=F32)


def _inproj_call(xn, w_main, w_dt):
    t = xn.shape[0]
    tm = _pick_tile(t, 512)
    row = lambda i: (i, 0)
    const = lambda i: (0, 0)
    return pl.pallas_call(
        _inproj_kernel,
        out_shape=(jax.ShapeDtypeStruct((t, ATTN_W), BF16),
                   jax.ShapeDtypeStruct((t, KVX_W), BF16),
                   jax.ShapeDtypeStruct((t, SSM_DI), BF16),
                   jax.ShapeDtypeStruct((t, CONV_DIM), BF16),
                   jax.ShapeDtypeStruct((t, BLK), F32)),
        grid=(t // tm,),
        in_specs=[pl.BlockSpec((tm, D_MODEL), row),
                  pl.BlockSpec((D_MODEL, W_MAIN), const),
                  pl.BlockSpec((D_MODEL, BLK), const)],
        out_specs=(pl.BlockSpec((tm, ATTN_W), row),
                   pl.BlockSpec((tm, KVX_W), row),
                   pl.BlockSpec((tm, SSM_DI), row),
                   pl.BlockSpec((tm, CONV_DIM), row),
                   pl.BlockSpec((tm, BLK), row)),
        compiler_params=_cparams(("arbitrary",)),
        name="in_proj",
    )(xn, w_main, w_dt)


def _pack_w_in(w):
    k0, k1 = w[:, SRC_K:SRC_K + HEAD_DIM], w[:, SRC_K + HEAD_DIM:SRC_V]
    v0, v1 = w[:, SRC_V:SRC_V + HEAD_DIM], w[:, SRC_V + HEAD_DIM:SRC_Z]
    zero = jnp.zeros_like(k0)
    w_main = jnp.concatenate([w[:, :SRC_K], k0, k0, k1, k1, v0, zero, v1, zero, w[:, SRC_Z:SRC_DT]], axis=1)
    w_dt = jnp.pad(w[:, SRC_DT:], ((0, 0), (0, BLK - SSM_HEADS)))
    return w_main.astype(BF16), w_dt.astype(BF16)


def _attn_bias():
    qi = np.arange(BLK)[:, None]
    c = np.arange(3 * BLK)[None, :]
    kb, ki = c // BLK, c % BLK
    out = []
    for n in range(3):
        qpos = n * BLK + qi - PAD
        vis_meta = (ki >= PAD) & (ki - PAD <= qpos)
        vis_prev = (ki > qi) & (n >= 2)
        vis_cur = (ki <= qi) & (n >= 1)
        vis = np.where(kb == 0, vis_meta, np.where(kb == 1, vis_prev, vis_cur))
        out.append(np.where(vis, 0.0, NEG))
    return np.stack(out).astype(np.float32)


def _attn_kernel(sink_ref, q_ref, kvc_ref, kvp_ref, kvm_ref, bias_ref, g_ref, o_ref):
    rep = ATTN_HEADS // KV_HEADS
    lo = lax.broadcasted_iota(jnp.int32, (BLK, BLK), 1) < HEAD_DIM
    srow = lax.broadcasted_iota(jnp.int32, (rep * BLK, 1), 0)
    bias = jnp.concatenate([bias_ref[...]] * rep, axis=0)
    nt = (((1,), (1,)), ((), ()))
    pairs = []
    for g in range(KV_HEADS):
        ksl = slice(BLK * g, BLK * (g + 1))
        vsl = slice(2 * KV_W + BLK * g, 2 * KV_W + BLK * (g + 1))
        qs = []
        for a in range(rep // 2):
            two = q_ref[:, rep * HEAD_DIM * g + BLK * a:rep * HEAD_DIM * g + BLK * (a + 1)]
            zero = jnp.zeros_like(two)
            qs += [jnp.where(lo, two, zero), jnp.where(lo, zero, two)]
        qs = jnp.concatenate(qs, axis=0)
        kcat = jnp.concatenate([kvm_ref[:, ksl], kvp_ref[:, ksl], kvc_ref[:, ksl]], axis=0)
        vcat = jnp.concatenate([kvm_ref[:, vsl], kvp_ref[:, vsl], kvc_ref[:, vsl]], axis=0)
        s = lax.dot_general(qs, kcat, nt, preferred_element_type=F32) + bias
        sink = sink_ref[rep * g + rep - 1]
        for a in range(rep - 2, -1, -1):
            sink = jnp.where(srow < BLK * (a + 1), sink_ref[rep * g + a], sink)
        m = jnp.maximum(jnp.max(s, axis=-1, keepdims=True), sink)
        p = jnp.exp(s - m).astype(BF16)
        o = jnp.dot(p, vcat, preferred_element_type=F32)
        y = o * pltpu.roll(1.0 / (o + jnp.exp(sink - m)), HEAD_DIM, 1)
        for a in range(rep // 2):
            even = y[2 * a * BLK:(2 * a + 1) * BLK]
            odd = y[(2 * a + 1) * BLK:(2 * a + 2) * BLK]
            pairs.append(jnp.where(lo, even, pltpu.roll(odd, HEAD_DIM, 1)))
    o_ref[...] = _rms(jnp.concatenate(pairs, axis=-1), g_ref[...]).astype(o_ref.dtype)


def _attn_call(q, kvx, sinks, gain, bsz, nb):
    t = q.shape[0]
    return pl.pallas_call(
        _attn_kernel,
        out_shape=jax.ShapeDtypeStruct((t, ATTN_W), BF16),
        grid=(bsz, nb),
        in_specs=[pl.BlockSpec(memory_space=pltpu.SMEM),
                  pl.BlockSpec((BLK, ATTN_W), lambda b, n: (b * nb + n, 0)),
                  pl.BlockSpec((BLK, KVX_W), lambda b, n: (b * nb + n, 0)),
                  pl.BlockSpec((BLK, KVX_W), lambda b, n: (b * nb + jnp.maximum(n - 1, 0), 0)),
                  pl.BlockSpec((BLK, KVX_W), lambda b, n: (b * nb, 0)),
                  pl.BlockSpec((None, BLK, 3 * BLK), lambda b, n: (jnp.minimum(n, 2), 0, 0)),
                  pl.BlockSpec((1, ATTN_W), lambda b, n: (0, 0))],
        out_specs=pl.BlockSpec((BLK, ATTN_W), lambda b, n: (b * nb + n, 0)),
        compiler_params=_cparams(("arbitrary", "arbitrary")),
        name="swa_attention",
    )(sinks, q, kvx, kvx, kvx, jnp.asarray(_attn_bias()), gain)


def _ssd_kernel(xbc_ref, z_ref, dt_ref, cw_ref, cb_ref, dtb_ref, alog_ref, dsk_ref, nw_ref,
                y_ref, tail_ref, state_ref, xd_ref, yd_ref, dout_ref):
    c = pl.program_id(1)

    @pl.when(c == 0)
    def _():
        tail_ref[...] = jnp.zeros_like(tail_ref)
        state_ref[...] = jnp.zeros_like(state_ref)

    valid = jnp.logical_or(c > 0, lax.broadcasted_iota(jnp.int32, (BLK, 1), 0) >= PAD)
    x = jnp.where(valid, xbc_ref[...].astype(F32), 0.0)
    tail_ref[SUBLANES:, :] = x
    acc = x * cw_ref[CONV_K - 1:CONV_K, :] + cb_ref[...]
    for s in range(1, CONV_K):
        acc = acc + tail_ref[SUBLANES - s:SUBLANES - s + BLK, :] * cw_ref[CONV_K - 1 - s:CONV_K - s, :]
    tail_ref[:SUBLANES, :] = x[BLK - SUBLANES:]
    act = acc * _sigmoid(acc)
    bm = act[:, SSM_DI:SSM_DI + SSM_G * SSM_N]
    cm = act[:, SSM_DI + SSM_G * SSM_N:]

    dtr = dt_ref[...] + dtb_ref[...]
    dt = jnp.maximum(dtr, 0.0) + jnp.log1p(jnp.exp(-jnp.abs(dtr)))
    dt = jnp.where(valid, dt, 0.0)
    da = dt * (-jnp.exp(alog_ref[...]))

    row = lax.broadcasted_iota(jnp.int32, (BLK, BLK), 0)
    col = lax.broadcasted_iota(jnp.int32, (BLK, BLK), 1)
    causal = row >= col
    tril = jnp.where(causal, 1.0, 0.0).astype(BF16)
    a_cs = sum(jnp.dot(tril, part, preferred_element_type=F32) for part in _split3(da))
    a_cs_t = a_cs.T
    d_state = jnp.exp(a_cs[BLK - 1:BLK, :] - a_cs)
    d_out = jnp.exp(a_cs)
    dtd = dt * d_state

    nt = (((1,), (1,)), ((), ()))
    cbs = [lax.dot_general(cm[:, SSM_N * g:SSM_N * (g + 1)].astype(BF16),
                           bm[:, SSM_N * g:SSM_N * (g + 1)].astype(BF16),
                           nt, preferred_element_type=F32) for g in range(SSM_G)]
    lo_half = col < SSM_P
    pairs = SSM_HEADS // 2
    for j in range(pairs):
        g = j // (pairs // SSM_G)
        h0, h1 = 2 * j, 2 * j + 1
        sl = slice(BLK * j, BLK * (j + 1))

        def per_head(v):
            return jnp.where(lo_half, v[:, h0:h0 + 1], v[:, h1:h1 + 1])

        xs_p = act[:, sl]
        xdt = xs_p * per_head(dt)
        xd_ref[:, sl] = (xs_p * per_head(dtd)).astype(BF16)
        dout_ref[:, sl] = per_head(d_out)

        def intra(h):
            seg = a_cs[:, h:h + 1] - a_cs_t[h:h + 1, :]
            return (cbs[g] * jnp.exp(jnp.where(causal, seg, NEG))).astype(BF16)

        yd_ref[:, sl] = (
            jnp.dot(intra(h0), jnp.where(lo_half, xdt, 0.0).astype(BF16), preferred_element_type=F32)
            + jnp.dot(intra(h1), jnp.where(lo_half, 0.0, xdt).astype(BF16), preferred_element_type=F32))

    for g in range(SSM_G):
        gs = slice(SSM_GW * g, SSM_GW * (g + 1))
        st = state_ref[g]
        cm_g = cm[:, SSM_N * g:SSM_N * (g + 1)].astype(BF16)
        bm_t = bm[:, SSM_N * g:SSM_N * (g + 1)].T.astype(BF16)
        y_off = jnp.dot(cm_g, st.astype(BF16), preferred_element_type=F32) * dout_ref[:, gs]
        state_ref[g] = (st * dout_ref[BLK - 1:BLK, gs]
                        + jnp.dot(bm_t, xd_ref[:, gs], preferred_element_type=F32))
        y = yd_ref[:, gs] + y_off + act[:, gs] * dsk_ref[:, gs]
        zz = z_ref[:, gs].astype(F32)
        y = y * (zz * _sigmoid(zz))
        y_ref[:, gs] = _rms(y, nw_ref[:, gs]).astype(y_ref.dtype)


def _ssd_call(xbc, z, dt_raw, conv_w, conv_b, dt_bias, a_log, d_skip, norm_w, bsz, nb):
    t = xbc.shape[0]
    blk = lambda b, n: (b * nb + n, 0)
    const = lambda b, n: (0, 0)
    return pl.pallas_call(
        _ssd_kernel,
        out_shape=jax.ShapeDtypeStruct((t, SSM_DI), BF16),
        grid=(bsz, nb),
        in_specs=[pl.BlockSpec((BLK, CONV_DIM), blk),
                  pl.BlockSpec((BLK, SSM_DI), blk),
                  pl.BlockSpec((BLK, BLK), blk),
                  pl.BlockSpec((CONV_K, CONV_DIM), const),
                  pl.BlockSpec((1, CONV_DIM), const),
                  pl.BlockSpec((1, BLK), const),
                  pl.BlockSpec((1, BLK), const),
                  pl.BlockSpec((1, SSM_DI), const),
                  pl.BlockSpec((1, SSM_DI), const)],
        out_specs=pl.BlockSpec((BLK, SSM_DI), blk),
        scratch_shapes=[pltpu.VMEM((SUBLANES + BLK, CONV_DIM), F32),
                        pltpu.VMEM((SSM_G, SSM_N, SSM_GW), F32),
                        pltpu.VMEM((BLK, SSM_DI), BF16),
                        pltpu.VMEM((BLK, SSM_DI), F32),
                        pltpu.VMEM((BLK, SSM_DI), F32)],
        compiler_params=_cparams(("arbitrary", "arbitrary")),
        name="conv_ssd",
    )(xbc, z, dt_raw, conv_w, conv_b, dt_bias, a_log, d_skip, norm_w)


def _outproj_kernel(ya_ref, ys_ref, h_ref, wa_ref, ws_ref, g_ref, hn_ref, u_ref):
    hn = (h_ref[...]
          + jnp.dot(ya_ref[...], wa_ref[...], preferred_element_type=F32)
          + jnp.dot(ys_ref[...], ws_ref[...], preferred_element_type=F32))
    hn_ref[...] = hn
    u_ref[...] = _rms(hn, g_ref[...]).astype(u_ref.dtype)


def _outproj_call(ya, ys, h, wa, ws, g, u_dtype):
    t = h.shape[0]
    tm = _pick_tile(t, 512)
    row = lambda i: (i, 0)
    const = lambda i: (0, 0)
    return pl.pallas_call(
        _outproj_kernel,
        out_shape=(jax.ShapeDtypeStruct((t, D_MODEL), F32),
                   jax.ShapeDtypeStruct((t, D_MODEL), u_dtype)),
        grid=(t // tm,),
        in_specs=[pl.BlockSpec((tm, ATTN_W), row),
                  pl.BlockSpec((tm, SSM_DI), row),
                  pl.BlockSpec((tm, D_MODEL), row),
                  pl.BlockSpec((ATTN_W, D_MODEL), const),
                  pl.BlockSpec((SSM_DI, D_MODEL), const),
                  pl.BlockSpec((1, D_MODEL), const)],
        out_specs=(pl.BlockSpec((tm, D_MODEL), row),
                   pl.BlockSpec((tm, D_MODEL), row)),
        compiler_params=_cparams(("arbitrary",)),
        name="out_proj",
    )(ya, ys, h, wa, ws, g)


def _ffn_kernel(u_ref, h_ref, w1_ref, w3_ref, w2_ref, g_ref, hn_ref, xn_ref, acc_ref):
    f = pl.program_id(1)

    @pl.when(f == 0)
    def _():
        acc_ref[...] = jnp.zeros_like(acc_ref)

    u = u_ref[...]
    a = jnp.dot(u, w1_ref[...], preferred_element_type=F32)
    b = jnp.dot(u, w3_ref[...], preferred_element_type=F32)
    act = (a * _sigmoid(a) * b).astype(BF16)
    acc_ref[...] += jnp.dot(act, w2_ref[...], preferred_element_type=F32)

    @pl.when(f == pl.num_programs(1) - 1)
    def _():
        hn = h_ref[...] + acc_ref[...]
        hn_ref[...] = hn
        xn_ref[...] = _rms(hn, g_ref[...]).astype(xn_ref.dtype)


def _ffn_call(u, h, w1, w3, w2, g):
    t = h.shape[0]
    d_ff = w1.shape[1]
    tm = _pick_tile(t, 1024)
    tf = _pick_tile(d_ff, 512)
    return pl.pallas_call(
        _ffn_kernel,
        out_shape=(jax.ShapeDtypeStruct((t, D_MODEL), F32),
                   jax.ShapeDtypeStruct((t, D_MODEL), BF16)),
        grid=(t // tm, d_ff // tf),
        in_specs=[pl.BlockSpec((tm, D_MODEL), lambda i, f: (i, 0)),
                  pl.BlockSpec((tm, D_MODEL), lambda i, f: (i, 0)),
                  pl.BlockSpec((D_MODEL, tf), lambda i, f: (0, f)),
                  pl.BlockSpec((D_MODEL, tf), lambda i, f: (0, f)),
                  pl.BlockSpec((tf, D_MODEL), lambda i, f: (f, 0)),
                  pl.BlockSpec((1, D_MODEL), lambda i, f: (0, 0))],
        out_specs=(pl.BlockSpec((tm, D_MODEL), lambda i, f: (i, 0)),
                   pl.BlockSpec((tm, D_MODEL), lambda i, f: (i, 0))),
        scratch_shapes=[pltpu.VMEM((tm, D_MODEL), F32)],
        compiler_params=_cparams(("arbitrary", "arbitrary")),
        name="dense_ffn",
    )(u, h, w1, w3, w2, g)


R_E1, R_E2, R_G1, R_G2, R_K1, R_K2 = range(6)


def _router_kernel(u_ref, rhi_ref, rlo_ref, route_ref, cnt_ref):
    tm = u_ref.shape[0]

    @pl.when(pl.program_id(0) == 0)
    def _():
        cnt_ref[...] = jnp.zeros_like(cnt_ref)

    u = u_ref[...]
    u_hi = u.astype(BF16)
    u_lo = (u - u_hi.astype(F32)).astype(BF16)
    logits = (jnp.dot(u_hi, rhi_ref[...], preferred_element_type=F32)
              + jnp.dot(u_lo, rhi_ref[...], preferred_element_type=F32)
              + jnp.dot(u_hi, rlo_ref[...], preferred_element_type=F32))
    lane = lax.broadcasted_iota(jnp.int32, (tm, BLK), 1)
    logits = jnp.where(lane < N_EXP, logits, -jnp.inf)
    m1 = jnp.max(logits, axis=-1, keepdims=True)
    i1 = jnp.min(jnp.where(logits == m1, lane, BLK), axis=-1, keepdims=True)
    rest = jnp.where(lane == i1, -jnp.inf, logits)
    m2 = jnp.max(rest, axis=-1, keepdims=True)
    i2 = jnp.min(jnp.where(rest == m2, lane, BLK), axis=-1, keepdims=True)
    e21 = jnp.exp(m2 - m1)
    g1 = 1.0 / (1.0 + e21)
    g2 = e21 / (1.0 + e21)

    sel1 = lane == i1
    sel2 = lane == i2
    onehot = jnp.where(sel1, 1.0, 0.0) + jnp.where(sel2, 1.0, 0.0)
    earlier = (lax.broadcasted_iota(jnp.int32, (tm, tm), 0)
               > lax.broadcasted_iota(jnp.int32, (tm, tm), 1))
    before = (jnp.dot(jnp.where(earlier, 1.0, 0.0).astype(BF16), onehot.astype(BF16),
                      preferred_element_type=F32) + cnt_ref[...])
    k1 = jnp.sum(jnp.where(sel1, before, 0.0), axis=-1, keepdims=True)
    k2 = jnp.sum(jnp.where(sel2, before, 0.0), axis=-1, keepdims=True)
    cnt_ref[...] += jnp.sum(onehot, axis=0, keepdims=True)

    rec = jnp.zeros((tm, BLK), F32)
    for idx, val in ((R_E1, i1.astype(F32)), (R_E2, i2.astype(F32)), (R_G1, g1), (R_G2, g2),
                     (R_K1, k1), (R_K2, k2)):
        rec = jnp.where(lane == idx, val, rec)
    route_ref[...] = rec


def _router_call(u, r_hi, r_lo):
    t = u.shape[0]
    tm = _pick_tile(t, 256)
    row = lambda i: (i, 0)
    const = lambda i: (0, 0)
    return pl.pallas_call(
        _router_kernel,
        out_shape=(jax.ShapeDtypeStruct((t, BLK), F32),
                   jax.ShapeDtypeStruct((1, BLK), F32)),
        grid=(t // tm,),
        in_specs=[pl.BlockSpec((tm, D_MODEL), row),
                  pl.BlockSpec((D_MODEL, BLK), const),
                  pl.BlockSpec((D_MODEL, BLK), const)],
        out_specs=(pl.BlockSpec((tm, BLK), row),
                   pl.BlockSpec((1, BLK), const)),
        compiler_params=_cparams(("arbitrary",)),
        name="router",
    )(u, r_hi, r_lo)


def _dispatch_kernel(zt_ref, pos_ref, u_ref, xs_hbm, zero_ref, sem, zsem, *, tm):
    n_fill = zt_ref.shape[0]

    def fill_copy(j):
        return pltpu.make_async_copy(zero_ref, xs_hbm.at[pl.ds(pl.multiple_of(zt_ref[j], MOE_TM), MOE_TM)], zsem)

    @pl.when(pl.program_id(0) == 0)
    def _():
        zero_ref[...] = jnp.zeros_like(zero_ref)
        for j in range(n_fill):
            fill_copy(j).start()
            fill_copy(j).wait()

    def row_copy(t, k):
        return pltpu.make_async_copy(u_ref.at[pl.ds(t, 1)], xs_hbm.at[pl.ds(pos_ref[2 * t + k], 1)], sem)

    def issue(t, carry):
        for k in range(2):
            row_copy(t, k).start(priority=k)
        return carry

    def drain(t, carry):
        for k in range(2):
            row_copy(t, k).wait()
        return carry

    lax.fori_loop(0, tm, issue, 0, unroll=DMA_UNROLL)
    lax.fori_loop(0, tm, drain, 0, unroll=DMA_UNROLL)


def _dispatch_call(fill_tiles, pos_flat, u, r_max):
    t = u.shape[0]
    tm = _pick_tile(t, 512)
    grid_spec = pltpu.PrefetchScalarGridSpec(
        num_scalar_prefetch=1,
        grid=(t // tm,),
        in_specs=[pl.BlockSpec((2 * tm,), lambda i, zt: (i,), memory_space=pltpu.SMEM),
                  pl.BlockSpec((tm, D_MODEL), lambda i, zt: (i, 0))],
        out_specs=pl.BlockSpec(memory_space=pl.ANY),
        scratch_shapes=[pltpu.VMEM((MOE_TM, D_MODEL), F32),
                        pltpu.SemaphoreType.DMA(()),
                        pltpu.SemaphoreType.DMA(())],
    )
    return pl.pallas_call(
        functools.partial(_dispatch_kernel, tm=tm),
        out_shape=jax.ShapeDtypeStruct((r_max, D_MODEL), F32),
        grid_spec=grid_spec,
        compiler_params=pltpu.CompilerParams(dimension_semantics=("arbitrary",),
                                             vmem_limit_bytes=VMEM_LIMIT, has_side_effects=True),
        name="moe_dispatch",
    )(fill_tiles, pos_flat, u)


def _moe_kernel(te_ref, nv_ref, x_ref, w1_ref, w3_ref, w2_ref, y_ref, acc_ref, xb_ref):
    del te_ref
    i = pl.program_id(0)
    f = pl.program_id(1)

    @pl.when(i < nv_ref[0])
    def _():
        @pl.when(f == 0)
        def _():
            acc_ref[...] = jnp.zeros_like(acc_ref)
            xb_ref[...] = x_ref[...].astype(BF16)

        x = xb_ref[...]
        a = jnp.dot(x, w1_ref[...], preferred_element_type=F32)
        b = jnp.dot(x, w3_ref[...], preferred_element_type=F32)
        act = (a * _sigmoid(a) * b).astype(BF16)
        acc_ref[...] += jnp.dot(act, w2_ref[...], preferred_element_type=F32)

        @pl.when(f == pl.num_programs(1) - 1)
        def _():
            y_ref[...] = acc_ref[...]

    @pl.when(jnp.logical_and(i >= nv_ref[0], f == 0))
    def _():
        y_ref[...] = jnp.zeros_like(y_ref)


def _moe_call(tile_expert, n_valid, xs, w1, w3, w2):
    r_max = xs.shape[0]
    d_ff = w1.shape[2]
    tm = MOE_TM
    tf = _pick_tile(d_ff, 512)
    nf = d_ff // tf

    def f_eff(i, f, nv):
        return jnp.where(i < nv[0], f, nf - 1)

    grid_spec = pltpu.PrefetchScalarGridSpec(
        num_scalar_prefetch=2,
        grid=(r_max // tm, nf),
        in_specs=[pl.BlockSpec((tm, D_MODEL), lambda i, f, te, nv: (i, 0)),
                  pl.BlockSpec((None, D_MODEL, tf), lambda i, f, te, nv: (te[i], 0, f_eff(i, f, nv))),
                  pl.BlockSpec((None, D_MODEL, tf), lambda i, f, te, nv: (te[i], 0, f_eff(i, f, nv))),
                  pl.BlockSpec((None, tf, D_MODEL), lambda i, f, te, nv: (te[i], f_eff(i, f, nv), 0))],
        out_specs=pl.BlockSpec((tm, D_MODEL), lambda i, f, te, nv: (i, 0)),
        scratch_shapes=[pltpu.VMEM((tm, D_MODEL), F32),
                        pltpu.VMEM((tm, D_MODEL), BF16)],
    )
    return pl.pallas_call(
        _moe_kernel,
        out_shape=jax.ShapeDtypeStruct((r_max, D_MODEL), F32),
        grid_spec=grid_spec,
        compiler_params=_cparams(("arbitrary", "arbitrary")),
        name="moe_ffn",
    )(tile_expert, n_valid, xs, w1, w3, w2)


def _combine_kernel(pos_ref, h_ref, route_ref, g_ref, ys_hbm, *rest, tm, emit_h):
    if emit_h:
        hn_ref, xn_ref, y0_ref, y1_ref, sem = rest
    else:
        xn_ref, y0_ref, y1_ref, sem = rest
    bufs = (y0_ref, y1_ref)

    def row_copy(t, k):
        return pltpu.make_async_copy(ys_hbm.at[pl.ds(pos_ref[2 * t + k], 1)], bufs[k].at[pl.ds(t, 1)], sem)

    def issue(t, carry):
        for k in range(2):
            row_copy(t, k).start(priority=k)
        return carry

    def drain(t, carry):
        for k in range(2):
            row_copy(t, k).wait()
        return carry

    lax.fori_loop(0, tm, issue, 0, unroll=DMA_UNROLL)
    lax.fori_loop(0, tm, drain, 0, unroll=DMA_UNROLL)
    hn = (h_ref[...]
          + route_ref[:, R_G1:R_G1 + 1] * y0_ref[...]
          + route_ref[:, R_G2:R_G2 + 1] * y1_ref[...])
    if emit_h:
        hn_ref[...] = hn
    xn_ref[...] = _rms(hn, g_ref[...]).astype(xn_ref.dtype)


def _combine_call(pos_flat, h, route, g, ys, emit_h, bsz, nb):
    t = h.shape[0]
    if emit_h:
        tm = _pick_tile(t, 256)
        grid = (t // tm,)
        row = lambda i: (i, 0)
        flat = lambda i: (i,)
        const = lambda i: (0, 0)
        out_shape = (jax.ShapeDtypeStruct((t, D_MODEL), F32), jax.ShapeDtypeStruct((t, D_MODEL), BF16))
        out_specs = (pl.BlockSpec((tm, D_MODEL), row), pl.BlockSpec((tm, D_MODEL), row))
    else:
        tm = BLK
        grid = (bsz, nb)
        row = lambda b, n: (b * nb + n, 0)
        flat = lambda b, n: (b * nb + n,)
        const = lambda b, n: (0, 0)
        out_shape = jax.ShapeDtypeStruct((bsz * (nb - 1) * BLK, D_MODEL), F32)
        out_specs = pl.BlockSpec((tm, D_MODEL), lambda b, n: (b * (nb - 1) + jnp.maximum(n - 1, 0), 0))
    return pl.pallas_call(
        functools.partial(_combine_kernel, tm=tm, emit_h=emit_h),
        out_shape=out_shape,
        grid=grid,
        in_specs=[pl.BlockSpec((2 * tm,), flat, memory_space=pltpu.SMEM),
                  pl.BlockSpec((tm, D_MODEL), row),
                  pl.BlockSpec((tm, BLK), row),
                  pl.BlockSpec((1, D_MODEL), const),
                  pl.BlockSpec(memory_space=pl.ANY)],
        out_specs=out_specs,
        scratch_shapes=[pltpu.VMEM((tm, D_MODEL), F32),
                        pltpu.VMEM((tm, D_MODEL), F32),
                        pltpu.SemaphoreType.DMA(())],
        compiler_params=_cparams(("arbitrary",) * len(grid)),
        name="moe_combine",
    )(pos_flat, h, route, g, ys)


def _moe_layer(h, u, router, w1, w3, w2, g_next, emit_h, bsz, nb):
    t = h.shape[0]
    tm = MOE_TM
    n_tiles = -(-(2 * t + N_EXP * (tm - 1)) // tm)
    r_max = n_tiles * tm

    r_pad = jnp.zeros((D_MODEL, BLK), F32).at[:, :N_EXP].set(router)
    r_hi = r_pad.astype(BF16)
    r_lo = (r_pad - r_hi.astype(F32)).astype(BF16)
    route, counts = _router_call(u, r_hi, r_lo)

    cnt = counts[0, :N_EXP].astype(jnp.int32)
    padded = (cnt + tm - 1) // tm * tm
    ends = jnp.cumsum(padded)
    starts = ends - padded
    experts = route[:, R_E1:R_E2 + 1].astype(jnp.int32)
    ranks = route[:, R_K1:R_K2 + 1].astype(jnp.int32)
    pos = (starts[experts] + ranks).reshape(-1)
    n_valid = (ends[N_EXP - 1] // tm).astype(jnp.int32).reshape(1)
    tile_start = jnp.minimum(jnp.arange(n_tiles, dtype=jnp.int32), n_valid[0] - 1) * tm
    tile_expert = jnp.sum(ends[None, :] <= tile_start[:, None], axis=1).astype(jnp.int32)
    idle = ends[N_EXP - 1] + jnp.arange(N_EXP, dtype=jnp.int32) * tm
    fill_tiles = jnp.clip(jnp.concatenate([ends - tm, idle]), 0, r_max - tm).astype(jnp.int32)

    xs = _dispatch_call(fill_tiles, pos, u, r_max)
    ys = _moe_call(tile_expert, n_valid, xs, w1, w3, w2)
    return _combine_call(pos, h, route, g_next, ys, emit_h, bsz, nb)


def kernel(x, meta_tokens, norm_mix, w_in, conv_w, conv_b, dt_bias, a_log, d_skip, ssm_norm,
           attn_norm, sinks, w_out, norm_ffn, ffn_w1, ffn_w3, ffn_w2, router, moe_w1, moe_w3,
           moe_w2, final_norm):
    bsz, seq, _ = x.shape
    depth = w_in.shape[0]
    assert seq % BLK == 0
    nb = seq // BLK + 1
    lp = nb * BLK

    def row(v, width=None):
        v = v.astype(F32).reshape(1, -1)
        if width is not None:
            v = jnp.pad(v, ((0, 0), (0, width - v.shape[1])))
        return v

    h, xn = _embed_call(x, meta_tokens, row(norm_mix[0]))
    out = None
    for i in range(depth):
        w_main, w_dt = _pack_w_in(w_in[i])
        q, kvx, z, xbc, dt_raw = _inproj_call(xn, w_main, w_dt)
        y_attn = _attn_call(q, kvx, sinks[i].astype(F32), row(attn_norm[i]), bsz, nb)
        y_ssm = _ssd_call(xbc, z, dt_raw, conv_w[i].T.astype(F32), row(conv_b[i]),
                          row(dt_bias[i], BLK), row(a_log[i], BLK),
                          row(jnp.repeat(d_skip[i], SSM_P)), row(ssm_norm[i]), bsz, nb)
        wo = w_out[i].astype(BF16)
        h, u = _outproj_call(y_attn, y_ssm, h, wo[:ATTN_W], wo[ATTN_W:], row(norm_ffn[i]),
                             BF16 if i % 2 == 0 else F32)
        last = i == depth - 1
        g_next = row(final_norm) if last else row(norm_mix[i + 1])
        j = i // 2
        if i % 2 == 0:
            h, xn = _ffn_call(u, h, ffn_w1[j].astype(BF16), ffn_w3[j].astype(BF16),
                              ffn_w2[j].astype(BF16), g_next)
            if last:
                out = _norm_call(h, g_next, F32)
        else:
            res = _moe_layer(h, u, router[j], moe_w1[j].astype(BF16), moe_w3[j].astype(BF16),
                             moe_w2[j].astype(BF16), g_next, not last, bsz, nb)
            if last:
                return res.reshape(bsz, seq, D_MODEL).astype(x.dtype)
            h, xn = res
    return out.reshape(bsz, lp, D_MODEL)[:, BLK:].astype(x.dtype)
```

```python
import functools

import numpy as np

import jax
import jax.numpy as jnp
from jax import lax
from jax.experimental import pallas as pl
from jax.experimental.pallas import tpu as pltpu

F32 = jnp.float32
BF16 = jnp.bfloat16

D_MODEL = 1024
N_META = 16
BLK = 128
SUBLANES = 8
PAD = BLK - N_META
ATTN_HEADS = 8
KV_HEADS = 2
HEAD_DIM = 64
ATTN_W = ATTN_HEADS * HEAD_DIM
KV_W = KV_HEADS * HEAD_DIM
SSM_HEADS = 24
SSM_P = 64
SSM_DI = SSM_HEADS * SSM_P
SSM_G = 2
SSM_N = 128
SSM_GW = SSM_DI // SSM_G
CONV_K = 4
CONV_DIM = SSM_DI + 2 * SSM_G * SSM_N
N_EXP = 8
EPS = 1e-6
NEG = -1e30

SRC_K = ATTN_W
SRC_V = ATTN_W + KV_W
SRC_Z = ATTN_W + 2 * KV_W
SRC_DT = SRC_Z + SSM_DI + CONV_DIM
KVX_W = 4 * KV_W
OFF_Q = 0
OFF_KVX = ATTN_W
OFF_Z = OFF_KVX + KVX_W
OFF_XBC = OFF_Z + SSM_DI
W_MAIN = OFF_XBC + CONV_DIM

MOE_TM = 512
FFN_TF = 512
WEIGHT_CHUNKS = 8
DMA_UNROLL = 8
VMEM_LIMIT = 48 * 1024 * 1024
MOE_VMEM_LIMIT = 56 * 1024 * 1024


def _pick_tile(n, pref):
    t = pref
    while n % t:
        t //= 2
    return t


def _cparams(sem):
    return pltpu.CompilerParams(dimension_semantics=sem, vmem_limit_bytes=VMEM_LIMIT)


def _rms(xf, g):
    return xf * lax.rsqrt(jnp.mean(xf * xf, axis=-1, keepdims=True) + EPS) * g


def _sigmoid(x):
    return 1.0 / (1.0 + jnp.exp(-x))


def _split3(v):
    hi = v.astype(BF16)
    r1 = v - hi.astype(F32)
    mid = r1.astype(BF16)
    lo = (r1 - mid.astype(F32)).astype(BF16)
    return hi, mid, lo


def _norm_kernel(h_ref, g_ref, o_ref):
    o_ref[...] = _rms(h_ref[...], g_ref[...]).astype(o_ref.dtype)


def _norm_call(h, g, out_dtype=None):
    out_dtype = BF16 if out_dtype is None else out_dtype
    t = h.shape[0]
    tm = _pick_tile(t, 512)
    return pl.pallas_call(
        _norm_kernel,
        out_shape=jax.ShapeDtypeStruct((t, D_MODEL), out_dtype),
        grid=(t // tm,),
        in_specs=[pl.BlockSpec((tm, D_MODEL), lambda i: (i, 0)),
                  pl.BlockSpec((1, D_MODEL), lambda i: (0, 0))],
        out_specs=pl.BlockSpec((tm, D_MODEL), lambda i: (i, 0)),
        compiler_params=_cparams(("arbitrary",)),
        name="rmsnorm",
    )(h, g)


def _embed_kernel(x_ref, head_ref, g_ref, h_ref, xn_ref):
    h = jnp.where(pl.program_id(1) == 0, head_ref[...], x_ref[...].astype(F32))
    h_ref[...] = h
    xn_ref[...] = _rms(h, g_ref[...]).astype(xn_ref.dtype)


def _embed_call(x, meta_tokens, g):
    bsz, seq, _ = x.shape
    nb = seq // BLK + 1
    t = bsz * nb * BLK
    head = jnp.pad(meta_tokens.astype(F32), ((PAD, 0), (0, 0)))
    blk = lambda b, n: (b * nb + n, 0)
    const = lambda b, n: (0, 0)
    return pl.pallas_call(
        _embed_kernel,
        out_shape=(jax.ShapeDtypeStruct((t, D_MODEL), F32),
                   jax.ShapeDtypeStruct((t, D_MODEL), BF16)),
        grid=(bsz, nb),
        in_specs=[pl.BlockSpec((BLK, D_MODEL), lambda b, n: (b * (nb - 1) + jnp.maximum(n - 1, 0), 0)),
                  pl.BlockSpec((BLK, D_MODEL), const),
                  pl.BlockSpec((1, D_MODEL), const)],
        out_specs=(pl.BlockSpec((BLK, D_MODEL), blk),
                   pl.BlockSpec((BLK, D_MODEL), blk)),
        compiler_params=_cparams(("arbitrary", "arbitrary")),
        name="embed",
    )(x.reshape(bsz * seq, D_MODEL), head, g)


def _inproj_kernel(x_ref, w_ref, wdt_ref, q_ref, kvx_ref, z_ref, xbc_ref, dt_ref):
    x = x_ref[...]

    def mm(lo, n):
        return jnp.dot(x, w_ref[:, lo:lo + n], preferred_element_type=F32)

    q_ref[...] = (mm(OFF_Q, ATTN_W) * (HEAD_DIM ** -0.5)).astype(BF16)
    lane = lax.broadcasted_iota(jnp.int32, (1, KVX_W), 1)
    ones = jnp.where(jnp.logical_and(lane >= 2 * KV_W, (lane & HEAD_DIM) != 0), 1.0, 0.0)
    kvx_ref[...] = (mm(OFF_KVX, KVX_W) + ones).astype(BF16)
    for c in range(SSM_DI // 512):
        z_ref[:, 512 * c:512 * (c + 1)] = mm(OFF_Z + 512 * c, 512).astype(BF16)
    for c in range(CONV_DIM // 512):
        xbc_ref[:, 512 * c:512 * (c + 1)] = mm(OFF_XBC + 512 * c, 512).astype(BF16)
    dt_ref[...] = jnp.dot(x, wdt_ref[...], preferred_element_type=F32)


def _inproj_call(xn, w_main, w_dt):
    t = xn.shape[0]
    tm = _pick_tile(t, 512)
    row = lambda i: (i, 0)
    const = lambda i: (0, 0)
    return pl.pallas_call(
        _inproj_kernel,
        out_shape=(jax.ShapeDtypeStruct((t, ATTN_W), BF16),
                   jax.ShapeDtypeStruct((t, KVX_W), BF16),
                   jax.ShapeDtypeStruct((t, SSM_DI), BF16),
                   jax.ShapeDtypeStruct((t, CONV_DIM), BF16),
                   jax.ShapeDtypeStruct((t, BLK), F32)),
        grid=(t // tm,),
        in_specs=[pl.BlockSpec((tm, D_MODEL), row),
                  pl.BlockSpec((D_MODEL, W_MAIN), const),
                  pl.BlockSpec((D_MODEL, BLK), const)],
        out_specs=(pl.BlockSpec((tm, ATTN_W), row),
                   pl.BlockSpec((tm, KVX_W), row),
                   pl.BlockSpec((tm, SSM_DI), row),
                   pl.BlockSpec((tm, CONV_DIM), row),
                   pl.BlockSpec((tm, BLK), row)),
        compiler_params=_cparams(("arbitrary",)),
        name="in_proj",
    )(xn, w_main, w_dt)


def _pack_w_in(w):
    k0, k1 = w[:, SRC_K:SRC_K + HEAD_DIM], w[:, SRC_K + HEAD_DIM:SRC_V]
    v0, v1 = w[:, SRC_V:SRC_V + HEAD_DIM], w[:, SRC_V + HEAD_DIM:SRC_Z]
    zero = jnp.zeros_like(k0)
    w_main = jnp.concatenate([w[:, :SRC_K], k0, k0, k1, k1, v0, zero, v1, zero, w[:, SRC_Z:SRC_DT]], axis=1)
    w_dt = jnp.pad(w[:, SRC_DT:], ((0, 0), (0, BLK - SSM_HEADS)))
    return w_main.astype(BF16), w_dt.astype(BF16)


def _attn_bias():
    qi = np.arange(BLK)[:, None]
    c = np.arange(3 * BLK)[None, :]
    kb, ki = c // BLK, c % BLK
    out = []
    for n in range(3):
        qpos = n * BLK + qi - PAD
        vis_meta = (ki >= PAD) & (ki - PAD <= qpos)
        vis_prev = (ki > qi) & (n >= 2)
        vis_cur = (ki <= qi) & (n >= 1)
        vis = np.where(kb == 0, vis_meta, np.where(kb == 1, vis_prev, vis_cur))
        out.append(np.where(vis, 0.0, NEG))
    return np.stack(out).astype(np.float32)


def _attn_kernel(sink_ref, q_ref, kvc_ref, kvp_ref, kvm_ref, bias_ref, g_ref, o_ref):
    rep = ATTN_HEADS // KV_HEADS
    lo = lax.broadcasted_iota(jnp.int32, (BLK, BLK), 1) < HEAD_DIM
    srow = lax.broadcasted_iota(jnp.int32, (rep * BLK, 1), 0)
    bias = jnp.concatenate([bias_ref[...]] * rep, axis=0)
    nt = (((1,), (1,)), ((), ()))
    pairs = []
    for g in range(KV_HEADS):
        ksl = slice(BLK * g, BLK * (g + 1))
        vsl = slice(2 * KV_W + BLK * g, 2 * KV_W + BLK * (g + 1))
        qs = []
        for a in range(rep // 2):
            two = q_ref[:, rep * HEAD_DIM * g + BLK * a:rep * HEAD_DIM * g + BLK * (a + 1)]
            zero = jnp.zeros_like(two)
            qs += [jnp.where(lo, two, zero), jnp.where(lo, zero, two)]
        qs = jnp.concatenate(qs, axis=0)
        kcat = jnp.concatenate([kvm_ref[:, ksl], kvp_ref[:, ksl], kvc_ref[:, ksl]], axis=0)
        vcat = jnp.concatenate([kvm_ref[:, vsl], kvp_ref[:, vsl], kvc_ref[:, vsl]], axis=0)
        s = lax.dot_general(qs, kcat, nt, preferred_element_type=F32) + bias
        sink = sink_ref[rep * g + rep - 1]
        for a in range(rep - 2, -1, -1):
            sink = jnp.where(srow < BLK * (a + 1), sink_ref[rep * g + a], sink)
        m = jnp.maximum(jnp.max(s, axis=-1, keepdims=True), sink)
        p = jnp.exp(s - m).astype(BF16)
        o = jnp.dot(p, vcat, preferred_element_type=F32)
        y = o * pltpu.roll(1.0 / (o + jnp.exp(sink - m)), HEAD_DIM, 1)
        for a in range(rep // 2):
            even = y[2 * a * BLK:(2 * a + 1) * BLK]
            odd = y[(2 * a + 1) * BLK:(2 * a + 2) * BLK]
            pairs.append(jnp.where(lo, even, pltpu.roll(odd, HEAD_DIM, 1)))
    o_ref[...] = _rms(jnp.concatenate(pairs, axis=-1), g_ref[...]).astype(o_ref.dtype)


def _attn_call(q, kvx, sinks, gain, bsz, nb):
    t = q.shape[0]
    return pl.pallas_call(
        _attn_kernel,
        out_shape=jax.ShapeDtypeStruct((t, ATTN_W), BF16),
        grid=(bsz, nb),
        in_specs=[pl.BlockSpec(memory_space=pltpu.SMEM),
                  pl.BlockSpec((BLK, ATTN_W), lambda b, n: (b * nb + n, 0)),
                  pl.BlockSpec((BLK, KVX_W), lambda b, n: (b * nb + n, 0)),
                  pl.BlockSpec((BLK, KVX_W), lambda b, n: (b * nb + jnp.maximum(n - 1, 0), 0)),
                  pl.BlockSpec((BLK, KVX_W), lambda b, n: (b * nb, 0)),
                  pl.BlockSpec((None, BLK, 3 * BLK), lambda b, n: (jnp.minimum(n, 2), 0, 0)),
                  pl.BlockSpec((1, ATTN_W), lambda b, n: (0, 0))],
        out_specs=pl.BlockSpec((BLK, ATTN_W), lambda b, n: (b * nb + n, 0)),
        compiler_params=_cparams(("arbitrary", "arbitrary")),
        name="swa_attention",
    )(sinks, q, kvx, kvx, kvx, jnp.asarray(_attn_bias()), gain)


def _ssd_kernel(xbc_ref, z_ref, dt_ref, cw_ref, cb_ref, dtb_ref, alog_ref, dsk_ref, nw_ref,
                y_ref, tail_ref, state_ref, xd_ref, yd_ref, dout_ref):
    c = pl.program_id(1)

    @pl.when(c == 0)
    def _():
        tail_ref[...] = jnp.zeros_like(tail_ref)
        state_ref[...] = jnp.zeros_like(state_ref)

    valid = jnp.logical_or(c > 0, lax.broadcasted_iota(jnp.int32, (BLK, 1), 0) >= PAD)
    tail_ref[SUBLANES:, :] = xbc_ref[...].astype(F32)

    @pl.when(c == 0)
    def _():
        tail_ref[SUBLANES:SUBLANES + PAD, :] = jnp.zeros((PAD, CONV_DIM), F32)

    x = tail_ref[SUBLANES:, :]
    acc = x * cw_ref[CONV_K - 1:CONV_K, :] + cb_ref[...]
    for s in range(1, CONV_K):
        acc = acc + tail_ref[SUBLANES - s:SUBLANES - s + BLK, :] * cw_ref[CONV_K - 1 - s:CONV_K - s, :]
    tail_ref[:SUBLANES, :] = x[BLK - SUBLANES:]
    act = acc * _sigmoid(acc)
    bm = act[:, SSM_DI:SSM_DI + SSM_G * SSM_N]
    cm = act[:, SSM_DI + SSM_G * SSM_N:]

    dtr = dt_ref[...] + dtb_ref[...]
    dt = jnp.maximum(dtr, 0.0) + jnp.log1p(jnp.exp(-jnp.abs(dtr)))
    dt = jnp.where(valid, dt, 0.0)
    da = dt * (-jnp.exp(alog_ref[...]))

    row = lax.broadcasted_iota(jnp.int32, (BLK, BLK), 0)
    col = lax.broadcasted_iota(jnp.int32, (BLK, BLK), 1)
    causal = row >= col
    tril = jnp.where(causal, 1.0, 0.0).astype(BF16)
    a_cs = sum(jnp.dot(tril, part, preferred_element_type=F32) for part in _split3(da))
    a_cs_t = a_cs.T
    d_state = jnp.exp(a_cs[BLK - 1:BLK, :] - a_cs)
    d_out = jnp.exp(a_cs)
    dtd = dt * d_state

    nt = (((1,), (1,)), ((), ()))
    cbs = [lax.dot_general(cm[:, SSM_N * g:SSM_N * (g + 1)].astype(BF16),
                           bm[:, SSM_N * g:SSM_N * (g + 1)].astype(BF16),
                           nt, preferred_element_type=F32) for g in range(SSM_G)]
    lo_half = col < SSM_P
    pairs = SSM_HEADS // 2
    for j in range(pairs):
        g = j // (pairs // SSM_G)
        h0, h1 = 2 * j, 2 * j + 1
        sl = slice(BLK * j, BLK * (j + 1))

        def per_head(v):
            return jnp.where(lo_half, v[:, h0:h0 + 1], v[:, h1:h1 + 1])

        xs_p = act[:, sl]
        xdt = xs_p * per_head(dt)
        xd_ref[:, sl] = (xs_p * per_head(dtd)).astype(BF16)
        dout_ref[:, sl] = per_head(d_out)

        def intra(h):
            seg = a_cs[:, h:h + 1] - a_cs_t[h:h + 1, :]
            return (cbs[g] * jnp.exp(jnp.where(causal, seg, NEG))).astype(BF16)

        yd_ref[:, sl] = (
            jnp.dot(intra(h0), jnp.where(lo_half, xdt, 0.0).astype(BF16), preferred_element_type=F32)
            + jnp.dot(intra(h1), jnp.where(lo_half, 0.0, xdt).astype(BF16), preferred_element_type=F32))

    for g in range(SSM_G):
        gs = slice(SSM_GW * g, SSM_GW * (g + 1))
        st = state_ref[g]
        cm_g = cm[:, SSM_N * g:SSM_N * (g + 1)].astype(BF16)
        bm_t = bm[:, SSM_N * g:SSM_N * (g + 1)].T.astype(BF16)
        y_off = jnp.dot(cm_g, st.astype(BF16), preferred_element_type=F32) * dout_ref[:, gs]
        state_ref[g] = (st * dout_ref[BLK - 1:BLK, gs]
                        + jnp.dot(bm_t, xd_ref[:, gs], preferred_element_type=F32))
        y = yd_ref[:, gs] + y_off + act[:, gs] * dsk_ref[:, gs]
        zz = z_ref[:, gs].astype(F32)
        y = y * (zz * _sigmoid(zz))
        y_ref[:, gs] = _rms(y, nw_ref[:, gs]).astype(y_ref.dtype)


def _ssd_call(xbc, z, dt_raw, conv_w, conv_b, dt_bias, a_log, d_skip, norm_w, bsz, nb):
    t = xbc.shape[0]
    blk = lambda b, n: (b * nb + n, 0)
    const = lambda b, n: (0, 0)
    return pl.pallas_call(
        _ssd_kernel,
        out_shape=jax.ShapeDtypeStruct((t, SSM_DI), BF16),
        grid=(bsz, nb),
        in_specs=[pl.BlockSpec((BLK, CONV_DIM), blk),
                  pl.BlockSpec((BLK, SSM_DI), blk),
                  pl.BlockSpec((BLK, BLK), blk),
                  pl.BlockSpec((CONV_K, CONV_DIM), const),
                  pl.BlockSpec((1, CONV_DIM), const),
                  pl.BlockSpec((1, BLK), const),
                  pl.BlockSpec((1, BLK), const),
                  pl.BlockSpec((1, SSM_DI), const),
                  pl.BlockSpec((1, SSM_DI), const)],
        out_specs=pl.BlockSpec((BLK, SSM_DI), blk),
        scratch_shapes=[pltpu.VMEM((SUBLANES + BLK, CONV_DIM), F32),
                        pltpu.VMEM((SSM_G, SSM_N, SSM_GW), F32),
                        pltpu.VMEM((BLK, SSM_DI), BF16),
                        pltpu.VMEM((BLK, SSM_DI), F32),
                        pltpu.VMEM((BLK, SSM_DI), F32)],
        compiler_params=_cparams(("arbitrary", "arbitrary")),
        name="conv_ssd",
    )(xbc, z, dt_raw, conv_w, conv_b, dt_bias, a_log, d_skip, norm_w)


def _outproj_kernel(ya_ref, ys_ref, h_ref, wa_ref, ws_ref, g_ref, hn_ref, u_ref):
    hn = (h_ref[...]
          + jnp.dot(ya_ref[...], wa_ref[...], preferred_element_type=F32)
          + jnp.dot(ys_ref[...], ws_ref[...], preferred_element_type=F32))
    hn_ref[...] = hn
    u_ref[...] = _rms(hn, g_ref[...]).astype(u_ref.dtype)


def _outproj_call(ya, ys, h, wa, ws, g, u_dtype):
    t = h.shape[0]
    tm = _pick_tile(t, 512)
    row = lambda i: (i, 0)
    const = lambda i: (0, 0)
    return pl.pallas_call(
        _outproj_kernel,
        out_shape=(jax.ShapeDtypeStruct((t, D_MODEL), F32),
                   jax.ShapeDtypeStruct((t, D_MODEL), u_dtype)),
        grid=(t // tm,),
        in_specs=[pl.BlockSpec((tm, ATTN_W), row),
                  pl.BlockSpec((tm, SSM_DI), row),
                  pl.BlockSpec((tm, D_MODEL), row),
                  pl.BlockSpec((ATTN_W, D_MODEL), const),
                  pl.BlockSpec((SSM_DI, D_MODEL), const),
                  pl.BlockSpec((1, D_MODEL), const)],
        out_specs=(pl.BlockSpec((tm, D_MODEL), row),
                   pl.BlockSpec((tm, D_MODEL), row)),
        compiler_params=_cparams(("arbitrary",)),
        name="out_proj",
    )(ya, ys, h, wa, ws, g)


def _swiglu_tile(x, w1_ref, w3_ref, w2_ref, act_ref):
    d_ff = w1_ref.shape[1]
    tf = _pick_tile(d_ff, FFN_TF)
    for c in range(d_ff // tf):
        sl = slice(tf * c, tf * (c + 1))
        a = jnp.dot(x, w1_ref[:, sl], preferred_element_type=F32)
        b = jnp.dot(x, w3_ref[:, sl], preferred_element_type=F32)
        act_ref[:, sl] = (a * _sigmoid(a) * b).astype(BF16)
    return jnp.dot(act_ref[...], w2_ref[...], preferred_element_type=F32)


def _ffn_kernel(u_ref, h_ref, w1_ref, w3_ref, w2_ref, g_ref, hn_ref, xn_ref, act_ref):
    hn = h_ref[...] + _swiglu_tile(u_ref[...], w1_ref, w3_ref, w2_ref, act_ref)
    hn_ref[...] = hn
    xn_ref[...] = _rms(hn, g_ref[...]).astype(xn_ref.dtype)


def _ffn_call(u, h, w1, w3, w2, g):
    t = h.shape[0]
    d_ff = w1.shape[1]
    tm = _pick_tile(t, 512)
    row = lambda i: (i, 0)
    const = lambda i: (0, 0)
    once = pl.Buffered(1)
    return pl.pallas_call(
        _ffn_kernel,
        out_shape=(jax.ShapeDtypeStruct((t, D_MODEL), F32),
                   jax.ShapeDtypeStruct((t, D_MODEL), BF16)),
        grid=(t // tm,),
        in_specs=[pl.BlockSpec((tm, D_MODEL), row),
                  pl.BlockSpec((tm, D_MODEL), row),
                  pl.BlockSpec((D_MODEL, d_ff), const, pipeline_mode=once),
                  pl.BlockSpec((D_MODEL, d_ff), const, pipeline_mode=once),
                  pl.BlockSpec((d_ff, D_MODEL), const, pipeline_mode=once),
                  pl.BlockSpec((1, D_MODEL), const)],
        out_specs=(pl.BlockSpec((tm, D_MODEL), row),
                   pl.BlockSpec((tm, D_MODEL), row)),
        scratch_shapes=[pltpu.VMEM((tm, d_ff), BF16)],
        compiler_params=_cparams(("arbitrary",)),
        name="dense_ffn",
    )(u, h, w1, w3, w2, g)


R_E1, R_E2, R_G1, R_G2, R_K1, R_K2 = range(6)


def _router_kernel(u_ref, rhi_ref, rlo_ref, route_ref, cnt_ref):
    tm = u_ref.shape[0]

    @pl.when(pl.program_id(0) == 0)
    def _():
        cnt_ref[...] = jnp.zeros_like(cnt_ref)

    u = u_ref[...]
    u_hi = u.astype(BF16)
    u_lo = (u - u_hi.astype(F32)).astype(BF16)
    logits = (jnp.dot(u_hi, rhi_ref[...], preferred_element_type=F32)
              + jnp.dot(u_lo, rhi_ref[...], preferred_element_type=F32)
              + jnp.dot(u_hi, rlo_ref[...], preferred_element_type=F32))
    lane = lax.broadcasted_iota(jnp.int32, (tm, BLK), 1)
    logits = jnp.where(lane < N_EXP, logits, -jnp.inf)
    m1 = jnp.max(logits, axis=-1, keepdims=True)
    i1 = jnp.min(jnp.where(logits == m1, lane, BLK), axis=-1, keepdims=True)
    rest = jnp.where(lane == i1, -jnp.inf, logits)
    m2 = jnp.max(rest, axis=-1, keepdims=True)
    i2 = jnp.min(jnp.where(rest == m2, lane, BLK), axis=-1, keepdims=True)
    e21 = jnp.exp(m2 - m1)
    g1 = 1.0 / (1.0 + e21)
    g2 = e21 / (1.0 + e21)

    sel1 = lane == i1
    sel2 = lane == i2
    onehot = jnp.where(sel1, 1.0, 0.0) + jnp.where(sel2, 1.0, 0.0)
    earlier = (lax.broadcasted_iota(jnp.int32, (tm, tm), 0)
               > lax.broadcasted_iota(jnp.int32, (tm, tm), 1))
    before = (jnp.dot(jnp.where(earlier, 1.0, 0.0).astype(BF16), onehot.astype(BF16),
                      preferred_element_type=F32) + cnt_ref[...])
    k1 = jnp.sum(jnp.where(sel1, before, 0.0), axis=-1, keepdims=True)
    k2 = jnp.sum(jnp.where(sel2, before, 0.0), axis=-1, keepdims=True)
    cnt_ref[...] += jnp.sum(onehot, axis=0, keepdims=True)

    rec = jnp.zeros((tm, BLK), F32)
    for idx, val in ((R_E1, i1.astype(F32)), (R_E2, i2.astype(F32)), (R_G1, g1), (R_G2, g2),
                     (R_K1, k1), (R_K2, k2)):
        rec = jnp.where(lane == idx, val, rec)
    route_ref[...] = rec


def _router_call(u, r_hi, r_lo):
    t = u.shape[0]
    tm = _pick_tile(t, 512)
    row = lambda i: (i, 0)
    const = lambda i: (0, 0)
    return pl.pallas_call(
        _router_kernel,
        out_shape=(jax.ShapeDtypeStruct((t, BLK), F32),
                   jax.ShapeDtypeStruct((1, BLK), F32)),
        grid=(t // tm,),
        in_specs=[pl.BlockSpec((tm, D_MODEL), row),
                  pl.BlockSpec((D_MODEL, BLK), const),
                  pl.BlockSpec((D_MODEL, BLK), const)],
        out_specs=(pl.BlockSpec((tm, BLK), row),
                   pl.BlockSpec((1, BLK), const)),
        compiler_params=_cparams(("arbitrary",)),
        name="router",
    )(u, r_hi, r_lo)


def _dispatch_kernel(zt_ref, pos_ref, u_ref, xs_hbm, zero_ref, sem, zsem, *, tm):
    n_fill = zt_ref.shape[0]

    def fill_copy(j):
        return pltpu.make_async_copy(zero_ref, xs_hbm.at[pl.ds(pl.multiple_of(zt_ref[j], MOE_TM), MOE_TM)], zsem)

    @pl.when(pl.program_id(0) == 0)
    def _():
        zero_ref[...] = jnp.zeros_like(zero_ref)
        for j in range(n_fill):
            fill_copy(j).start()
            fill_copy(j).wait()

    def row_copy(t, k):
        return pltpu.make_async_copy(u_ref.at[pl.ds(t, 1)], xs_hbm.at[pl.ds(pos_ref[2 * t + k], 1)], sem)

    def issue(t, carry):
        for k in range(2):
            row_copy(t, k).start(priority=k)
        return carry

    def drain(t, carry):
        for k in range(2):
            row_copy(t, k).wait()
        return carry

    lax.fori_loop(0, tm, issue, 0, unroll=DMA_UNROLL)
    lax.fori_loop(0, tm, drain, 0, unroll=DMA_UNROLL)


def _dispatch_call(fill_tiles, pos_flat, u, r_max):
    t = u.shape[0]
    tm = _pick_tile(t, 512)
    grid_spec = pltpu.PrefetchScalarGridSpec(
        num_scalar_prefetch=1,
        grid=(t // tm,),
        in_specs=[pl.BlockSpec((2 * tm,), lambda i, zt: (i,), memory_space=pltpu.SMEM),
                  pl.BlockSpec((tm, D_MODEL), lambda i, zt: (i, 0))],
        out_specs=pl.BlockSpec(memory_space=pl.ANY),
        scratch_shapes=[pltpu.VMEM((MOE_TM, D_MODEL), F32),
                        pltpu.SemaphoreType.DMA(()),
                        pltpu.SemaphoreType.DMA(())],
    )
    return pl.pallas_call(
        functools.partial(_dispatch_kernel, tm=tm),
        out_shape=jax.ShapeDtypeStruct((r_max, D_MODEL), F32),
        grid_spec=grid_spec,
        compiler_params=pltpu.CompilerParams(dimension_semantics=("arbitrary",),
                                             vmem_limit_bytes=VMEM_LIMIT, has_side_effects=True),
        name="moe_dispatch",
    )(fill_tiles, pos_flat, u)


def _moe_kernel(te_ref, nv_ref, x_ref, w1_hbm, w3_hbm, w2_hbm, y_ref,
                w1_ref, w3_ref, w2_ref, up_stage, down_stage, act_ref, sem, *, layer):
    i = pl.program_id(0)
    e = te_ref[i]
    d_ff = w1_ref.shape[1]
    up_rows, down_rows = up_stage.shape[1], down_stage.shape[1]
    plan = ([(w1_hbm, w1_ref, up_stage, r) for r in range(0, D_MODEL, up_rows)]
            + [(w3_hbm, w3_ref, up_stage, r) for r in range(0, D_MODEL, up_rows)]
            + [(w2_hbm, w2_ref, down_stage, r) for r in range(0, d_ff, down_rows)])

    def chunk_copy(j):
        src, _, stage, r0 = plan[j]
        return pltpu.make_async_copy(src.at[layer, e, pl.ds(r0, stage.shape[1])], stage.at[j % 2], sem.at[j % 2])

    @pl.when(jnp.logical_or(i == 0, e != te_ref[jnp.maximum(i - 1, 0)]))
    def _():
        chunk_copy(0).start()
        for j in range(len(plan)):
            if j + 1 < len(plan):
                chunk_copy(j + 1).start()
            chunk_copy(j).wait()
            _, home, stage, r0 = plan[j]
            home[r0:r0 + stage.shape[1], :] = stage[j % 2].astype(BF16)

    @pl.when(i < nv_ref[0])
    def _():
        y_ref[...] = _swiglu_tile(x_ref[...].astype(BF16), w1_ref, w3_ref, w2_ref, act_ref)

    @pl.when(i >= nv_ref[0])
    def _():
        y_ref[...] = jnp.zeros_like(y_ref)


def _moe_call(tile_expert, n_valid, xs, w1, w3, w2, layer):
    r_max = xs.shape[0]
    d_ff = w1.shape[3]
    tm = MOE_TM
    grid_spec = pltpu.PrefetchScalarGridSpec(
        num_scalar_prefetch=2,
        grid=(r_max // tm,),
        in_specs=[pl.BlockSpec((tm, D_MODEL), lambda i, te, nv: (i, 0)),
                  pl.BlockSpec(memory_space=pl.ANY),
                  pl.BlockSpec(memory_space=pl.ANY),
                  pl.BlockSpec(memory_space=pl.ANY)],
        out_specs=pl.BlockSpec((tm, D_MODEL), lambda i, te, nv: (i, 0)),
        scratch_shapes=[pltpu.VMEM((D_MODEL, d_ff), BF16),
                        pltpu.VMEM((D_MODEL, d_ff), BF16),
                        pltpu.VMEM((d_ff, D_MODEL), BF16),
                        pltpu.VMEM((2, D_MODEL // WEIGHT_CHUNKS, d_ff), F32),
                        pltpu.VMEM((2, d_ff // WEIGHT_CHUNKS, D_MODEL), F32),
                        pltpu.VMEM((tm, d_ff), BF16),
                        pltpu.SemaphoreType.DMA((2,))],
    )
    return pl.pallas_call(
        functools.partial(_moe_kernel, layer=layer),
        out_shape=jax.ShapeDtypeStruct((r_max, D_MODEL), F32),
        grid_spec=grid_spec,
        compiler_params=pltpu.CompilerParams(dimension_semantics=("arbitrary",),
                                             vmem_limit_bytes=MOE_VMEM_LIMIT),
        name="moe_ffn",
    )(tile_expert, n_valid, xs, w1, w3, w2)


def _combine_kernel(pos_ref, h_ref, route_ref, g_ref, ys_hbm, *rest, tm, emit_h):
    if emit_h:
        hn_ref, xn_ref, y0_ref, y1_ref, sem = rest
    else:
        xn_ref, y0_ref, y1_ref, sem = rest
    bufs = (y0_ref, y1_ref)

    def row_copy(t, k):
        return pltpu.make_async_copy(ys_hbm.at[pl.ds(pos_ref[2 * t + k], 1)], bufs[k].at[pl.ds(t, 1)], sem)

    def issue(t, carry):
        for k in range(2):
            row_copy(t, k).start(priority=k)
        return carry

    def drain(t, carry):
        for k in range(2):
            row_copy(t, k).wait()
        return carry

    lax.fori_loop(0, tm, issue, 0, unroll=DMA_UNROLL)
    lax.fori_loop(0, tm, drain, 0, unroll=DMA_UNROLL)
    hn = (h_ref[...]
          + route_ref[:, R_G1:R_G1 + 1] * y0_ref[...]
          + route_ref[:, R_G2:R_G2 + 1] * y1_ref[...])
    if emit_h:
        hn_ref[...] = hn
    xn_ref[...] = _rms(hn, g_ref[...]).astype(xn_ref.dtype)


def _combine_call(pos_flat, h, route, g, ys, emit_h, bsz, nb):
    t = h.shape[0]
    if emit_h:
        tm = _pick_tile(t, 256)
        grid = (t // tm,)
        row = lambda i: (i, 0)
        flat = lambda i: (i,)
        const = lambda i: (0, 0)
        out_shape = (jax.ShapeDtypeStruct((t, D_MODEL), F32), jax.ShapeDtypeStruct((t, D_MODEL), BF16))
        out_specs = (pl.BlockSpec((tm, D_MODEL), row), pl.BlockSpec((tm, D_MODEL), row))
    else:
        tm = BLK
        grid = (bsz, nb)
        row = lambda b, n: (b * nb + n, 0)
        flat = lambda b, n: (b * nb + n,)
        const = lambda b, n: (0, 0)
        out_shape = jax.ShapeDtypeStruct((bsz * (nb - 1) * BLK, D_MODEL), F32)
        out_specs = pl.BlockSpec((tm, D_MODEL), lambda b, n: (b * (nb - 1) + jnp.maximum(n - 1, 0), 0))
    return pl.pallas_call(
        functools.partial(_combine_kernel, tm=tm, emit_h=emit_h),
        out_shape=out_shape,
        grid=grid,
        in_specs=[pl.BlockSpec((2 * tm,), flat, memory_space=pltpu.SMEM),
                  pl.BlockSpec((tm, D_MODEL), row),
                  pl.BlockSpec((tm, BLK), row),
                  pl.BlockSpec((1, D_MODEL), const),
                  pl.BlockSpec(memory_space=pl.ANY)],
        out_specs=out_specs,
        scratch_shapes=[pltpu.VMEM((tm, D_MODEL), F32),
                        pltpu.VMEM((tm, D_MODEL), F32),
                        pltpu.SemaphoreType.DMA(())],
        compiler_params=_cparams(("arbitrary",) * len(grid)),
        name="moe_combine",
    )(pos_flat, h, route, g, ys)


def _moe_layer(h, u, router, w1, w3, w2, layer, g_next, emit_h, bsz, nb):
    t = h.shape[0]
    tm = MOE_TM
    n_tiles = -(-(2 * t + N_EXP * (tm - 1)) // tm)
    r_max = n_tiles * tm

    r_pad = jnp.zeros((D_MODEL, BLK), F32).at[:, :N_EXP].set(router)
    r_hi = r_pad.astype(BF16)
    r_lo = (r_pad - r_hi.astype(F32)).astype(BF16)
    route, counts = _router_call(u, r_hi, r_lo)

    cnt = counts[0, :N_EXP].astype(jnp.int32)
    padded = (cnt + tm - 1) // tm * tm
    ends = jnp.cumsum(padded)
    starts = ends - padded
    experts = route[:, R_E1:R_E2 + 1].astype(jnp.int32)
    ranks = route[:, R_K1:R_K2 + 1].astype(jnp.int32)
    pos = (starts[experts] + ranks).reshape(-1)
    n_valid = (ends[N_EXP - 1] // tm).astype(jnp.int32).reshape(1)
    tile_start = jnp.minimum(jnp.arange(n_tiles, dtype=jnp.int32), n_valid[0] - 1) * tm
    tile_expert = jnp.sum(ends[None, :] <= tile_start[:, None], axis=1).astype(jnp.int32)
    idle = ends[N_EXP - 1] + jnp.arange(N_EXP, dtype=jnp.int32) * tm
    fill_tiles = jnp.clip(jnp.concatenate([ends - tm, idle]), 0, r_max - tm).astype(jnp.int32)

    xs = _dispatch_call(fill_tiles, pos, u, r_max)
    ys = _moe_call(tile_expert, n_valid, xs, w1, w3, w2, layer)
    return _combine_call(pos, h, route, g_next, ys, emit_h, bsz, nb)


def kernel(x, meta_tokens, norm_mix, w_in, conv_w, conv_b, dt_bias, a_log, d_skip, ssm_norm,
           attn_norm, sinks, w_out, norm_ffn, ffn_w1, ffn_w3, ffn_w2, router, moe_w1, moe_w3,
           moe_w2, final_norm):
    bsz, seq, _ = x.shape
    depth = w_in.shape[0]
    assert seq % BLK == 0
    nb = seq // BLK + 1
    lp = nb * BLK

    def row(v, width=None):
        v = v.astype(F32).reshape(1, -1)
        if width is not None:
            v = jnp.pad(v, ((0, 0), (0, width - v.shape[1])))
        return v

    h, xn = _embed_call(x, meta_tokens, row(norm_mix[0]))
    out = None
    for i in range(depth):
        w_main, w_dt = _pack_w_in(w_in[i])
        q, kvx, z, xbc, dt_raw = _inproj_call(xn, w_main, w_dt)
        y_attn = _attn_call(q, kvx, sinks[i].astype(F32), row(attn_norm[i]), bsz, nb)
        y_ssm = _ssd_call(xbc, z, dt_raw, conv_w[i].T.astype(F32), row(conv_b[i]),
                          row(dt_bias[i], BLK), row(a_log[i], BLK),
                          row(jnp.repeat(d_skip[i], SSM_P)), row(ssm_norm[i]), bsz, nb)
        wo = w_out[i].astype(BF16)
        h, u = _outproj_call(y_attn, y_ssm, h, wo[:ATTN_W], wo[ATTN_W:], row(norm_ffn[i]),
                             BF16 if i % 2 == 0 else F32)
        last = i == depth - 1
        g_next = row(final_norm) if last else row(norm_mix[i + 1])
        j = i // 2
        if i % 2 == 0:
            h, xn = _ffn_call(u, h, ffn_w1[j].astype(BF16), ffn_w3[j].astype(BF16),
                              ffn_w2[j].astype(BF16), g_next)
            if last:
                out = _norm_call(h, g_next, F32)
        else:
            res = _moe_layer(h, u, router[j], moe_w1.astype(F32), moe_w3.astype(F32),
                             moe_w2.astype(F32), j, g_next, not last, bsz, nb)
            if last:
                return res.reshape(bsz, seq, D_MODEL).astype(x.dtype)
            h, xn = res
    return out.reshape(bsz, lp, D_MODEL)[:, BLK:].astype(x.dtype)
```

```python
import functools

import numpy as np

import jax
import jax.numpy as jnp
from jax import lax
from jax.experimental import pallas as pl
from jax.experimental.pallas import tpu as pltpu

F32 = jnp.float32
BF16 = jnp.bfloat16

D_MODEL = 1024
N_META = 16
BLK = 128
SUBLANES = 8
PAD = BLK - N_META
ATTN_HEADS = 8
KV_HEADS = 2
HEAD_DIM = 64
ATTN_W = ATTN_HEADS * HEAD_DIM
KV_W = KV_HEADS * HEAD_DIM
SSM_HEADS = 24
SSM_P = 64
SSM_DI = SSM_HEADS * SSM_P
SSM_G = 2
SSM_N = 128
SSM_GW = SSM_DI // SSM_G
CONV_K = 4
CONV_DIM = SSM_DI + 2 * SSM_G * SSM_N
N_EXP = 8
EPS = 1e-6
NEG = -1e30

SRC_K = ATTN_W
SRC_V = ATTN_W + KV_W
SRC_Z = ATTN_W + 2 * KV_W
SRC_DT = SRC_Z + SSM_DI + CONV_DIM
KVX_W = 4 * KV_W
OFF_Q = 0
OFF_KVX = ATTN_W
OFF_Z = OFF_KVX + KVX_W
OFF_XBC = OFF_Z + SSM_DI
W_MAIN = OFF_XBC + CONV_DIM

ATTN_QB = 3
MOE_TM = 512
FFN_TF = 512
WEIGHT_CHUNKS = 8
DMA_UNROLL = 8
VMEM_LIMIT = 48 * 1024 * 1024
MOE_VMEM_LIMIT = 56 * 1024 * 1024


def _pick_tile(n, pref):
    t = pref
    while n % t:
        t //= 2
    return t


def _cparams(sem):
    return pltpu.CompilerParams(dimension_semantics=sem, vmem_limit_bytes=VMEM_LIMIT)


def _rms(xf, g):
    return xf * lax.rsqrt(jnp.mean(xf * xf, axis=-1, keepdims=True) + EPS) * g


def _sigmoid(x):
    return 1.0 / (1.0 + jnp.exp(-x))


def _split3(v):
    hi = v.astype(BF16)
    r1 = v - hi.astype(F32)
    mid = r1.astype(BF16)
    lo = (r1 - mid.astype(F32)).astype(BF16)
    return hi, mid, lo


def _norm_kernel(h_ref, g_ref, o_ref):
    o_ref[...] = _rms(h_ref[...], g_ref[...]).astype(o_ref.dtype)


def _norm_call(h, g, out_dtype=None):
    out_dtype = BF16 if out_dtype is None else out_dtype
    t = h.shape[0]
    tm = _pick_tile(t, 512)
    return pl.pallas_call(
        _norm_kernel,
        out_shape=jax.ShapeDtypeStruct((t, D_MODEL), out_dtype),
        grid=(t // tm,),
        in_specs=[pl.BlockSpec((tm, D_MODEL), lambda i: (i, 0)),
                  pl.BlockSpec((1, D_MODEL), lambda i: (0, 0))],
        out_specs=pl.BlockSpec((tm, D_MODEL), lambda i: (i, 0)),
        compiler_params=_cparams(("arbitrary",)),
        name="rmsnorm",
    )(h, g)


def _embed_kernel(x_ref, head_ref, g_ref, h_ref, xn_ref):
    h = jnp.where(pl.program_id(1) == 0, head_ref[...], x_ref[...].astype(F32))
    h_ref[...] = h
    xn_ref[...] = _rms(h, g_ref[...]).astype(xn_ref.dtype)


def _embed_call(x, meta_tokens, g):
    bsz, seq, _ = x.shape
    nb = seq // BLK + 1
    t = bsz * nb * BLK
    head = jnp.pad(meta_tokens.astype(F32), ((PAD, 0), (0, 0)))
    blk = lambda b, n: (b * nb + n, 0)
    const = lambda b, n: (0, 0)
    return pl.pallas_call(
        _embed_kernel,
        out_shape=(jax.ShapeDtypeStruct((t, D_MODEL), F32),
                   jax.ShapeDtypeStruct((t, D_MODEL), BF16)),
        grid=(bsz, nb),
        in_specs=[pl.BlockSpec((BLK, D_MODEL), lambda b, n: (b * (nb - 1) + jnp.maximum(n - 1, 0), 0)),
                  pl.BlockSpec((BLK, D_MODEL), const),
                  pl.BlockSpec((1, D_MODEL), const)],
        out_specs=(pl.BlockSpec((BLK, D_MODEL), blk),
                   pl.BlockSpec((BLK, D_MODEL), blk)),
        compiler_params=_cparams(("arbitrary", "arbitrary")),
        name="embed",
    )(x.reshape(bsz * seq, D_MODEL), head, g)


def _inproj_kernel(x_ref, w_ref, wdt_ref, q_ref, kvx_ref, z_ref, xbc_ref, dt_ref):
    x = x_ref[...]

    def mm(lo, n):
        return jnp.dot(x, w_ref[:, lo:lo + n], preferred_element_type=F32)

    q_ref[...] = (mm(OFF_Q, ATTN_W) * (HEAD_DIM ** -0.5)).astype(BF16)
    lane = lax.broadcasted_iota(jnp.int32, (1, KVX_W), 1)
    ones = jnp.where(jnp.logical_and(lane >= 2 * KV_W, (lane & HEAD_DIM) != 0), 1.0, 0.0)
    kvx_ref[...] = (mm(OFF_KVX, KVX_W) + ones).astype(BF16)
    for c in range(SSM_DI // 512):
        z_ref[:, 512 * c:512 * (c + 1)] = mm(OFF_Z + 512 * c, 512).astype(BF16)
    for c in range(CONV_DIM // 512):
        xbc_ref[:, 512 * c:512 * (c + 1)] = mm(OFF_XBC + 512 * c, 512).astype(BF16)
    dt_ref[...] = jnp.dot(x, wdt_ref[...], preferred_element_type=F32)


def _inproj_call(xn, w_main, w_dt):
    t = xn.shape[0]
    tm = _pick_tile(t, 512)
    row = lambda i: (i, 0)
    const = lambda i: (0, 0)
    return pl.pallas_call(
        _inproj_kernel,
        out_shape=(jax.ShapeDtypeStruct((t, ATTN_W), BF16),
                   jax.ShapeDtypeStruct((t, KVX_W), BF16),
                   jax.ShapeDtypeStruct((t, SSM_DI), BF16),
                   jax.ShapeDtypeStruct((t, CONV_DIM), BF16),
                   jax.ShapeDtypeStruct((t, BLK), F32)),
        grid=(t // tm,),
        in_specs=[pl.BlockSpec((tm, D_MODEL), row),
                  pl.BlockSpec((D_MODEL, W_MAIN), const),
                  pl.BlockSpec((D_MODEL, BLK), const)],
        out_specs=(pl.BlockSpec((tm, ATTN_W), row),
                   pl.BlockSpec((tm, KVX_W), row),
                   pl.BlockSpec((tm, SSM_DI), row),
                   pl.BlockSpec((tm, CONV_DIM), row),
                   pl.BlockSpec((tm, BLK), row)),
        compiler_params=_cparams(("arbitrary",)),
        name="in_proj",
    )(xn, w_main, w_dt)


def _pack_w_in(w):
    k0, k1 = w[:, SRC_K:SRC_K + HEAD_DIM], w[:, SRC_K + HEAD_DIM:SRC_V]
    v0, v1 = w[:, SRC_V:SRC_V + HEAD_DIM], w[:, SRC_V + HEAD_DIM:SRC_Z]
    zero = jnp.zeros_like(k0)
    w_main = jnp.concatenate([w[:, :SRC_K], k0, k0, k1, k1, v0, zero, v1, zero, w[:, SRC_Z:SRC_DT]], axis=1)
    w_dt = jnp.pad(w[:, SRC_DT:], ((0, 0), (0, BLK - SSM_HEADS)))
    return w_main.astype(BF16), w_dt.astype(BF16)


def _attn_bias():
    qi = np.arange(BLK)[:, None]
    c = np.arange(3 * BLK)[None, :]
    kb, ki = c // BLK, c % BLK
    out = []
    for n in range(3):
        qpos = n * BLK + qi - PAD
        vis_meta = (ki >= PAD) & (ki - PAD <= qpos)
        vis_prev = (ki > qi) & (n >= 2)
        vis_cur = (ki <= qi) & (n >= 1)
        vis = np.where(kb == 0, vis_meta, np.where(kb == 1, vis_prev, vis_cur))
        out.append(np.where(vis, 0.0, NEG))
    return np.stack(out).astype(np.float32)


def _attn_kernel(sink_ref, q_ref, kvc_ref, kvp_ref, kvm_ref, bias_ref, g_ref, o_ref):
    rep = ATTN_HEADS // KV_HEADS
    lo = lax.broadcasted_iota(jnp.int32, (BLK, BLK), 1) < HEAD_DIM
    srow = lax.broadcasted_iota(jnp.int32, (rep * BLK, 1), 0)
    nt = (((1,), (1,)), ((), ()))
    for i in range(q_ref.shape[0] // BLK):
        rows = slice(BLK * i, BLK * (i + 1))
        before = slice(BLK * (i - 1), BLK * i)
        bias = jnp.concatenate([bias_ref[i]] * rep, axis=0)
        pairs = []
        for g in range(KV_HEADS):
            ksl = slice(BLK * g, BLK * (g + 1))
            vsl = slice(2 * KV_W + BLK * g, 2 * KV_W + BLK * (g + 1))
            qs = []
            for a in range(rep // 2):
                two = q_ref[rows, rep * HEAD_DIM * g + BLK * a:rep * HEAD_DIM * g + BLK * (a + 1)]
                zero = jnp.zeros_like(two)
                qs += [jnp.where(lo, two, zero), jnp.where(lo, zero, two)]
            qs = jnp.concatenate(qs, axis=0)
            k_prev, v_prev = (kvp_ref[:, ksl], kvp_ref[:, vsl]) if i == 0 else (kvc_ref[before, ksl], kvc_ref[before, vsl])
            kcat = jnp.concatenate([kvm_ref[:, ksl], k_prev, kvc_ref[rows, ksl]], axis=0)
            vcat = jnp.concatenate([kvm_ref[:, vsl], v_prev, kvc_ref[rows, vsl]], axis=0)
            s = lax.dot_general(qs, kcat, nt, preferred_element_type=F32) + bias
            sink = sink_ref[rep * g + rep - 1]
            for a in range(rep - 2, -1, -1):
                sink = jnp.where(srow < BLK * (a + 1), sink_ref[rep * g + a], sink)
            m = jnp.maximum(jnp.max(s, axis=-1, keepdims=True), sink)
            p = jnp.exp(s - m).astype(BF16)
            o = jnp.dot(p, vcat, preferred_element_type=F32)
            y = o * pltpu.roll(1.0 / (o + jnp.exp(sink - m)), HEAD_DIM, 1)
            for a in range(rep // 2):
                even = y[2 * a * BLK:(2 * a + 1) * BLK]
                odd = y[(2 * a + 1) * BLK:(2 * a + 2) * BLK]
                pairs.append(jnp.where(lo, even, pltpu.roll(odd, HEAD_DIM, 1)))
        o_ref[rows, :] = _rms(jnp.concatenate(pairs, axis=-1), g_ref[...]).astype(o_ref.dtype)


def _attn_call(q, kvx, sinks, gain, bsz, nb):
    t = q.shape[0]
    qb = ATTN_QB if nb % ATTN_QB == 0 else 1
    steps = nb // qb
    masks = _attn_bias()
    table = np.stack([masks[[min(v * qb + i, 2) for i in range(qb)]] for v in range(3)])
    tile = lambda b, j: (b * steps + j, 0)
    return pl.pallas_call(
        _attn_kernel,
        out_shape=jax.ShapeDtypeStruct((t, ATTN_W), BF16),
        grid=(bsz, steps),
        in_specs=[pl.BlockSpec(memory_space=pltpu.SMEM),
                  pl.BlockSpec((qb * BLK, ATTN_W), tile),
                  pl.BlockSpec((qb * BLK, KVX_W), tile),
                  pl.BlockSpec((BLK, KVX_W), lambda b, j: (b * nb + jnp.maximum(j * qb - 1, 0), 0)),
                  pl.BlockSpec((BLK, KVX_W), lambda b, j: (b * nb, 0)),
                  pl.BlockSpec((None, qb, BLK, 3 * BLK), lambda b, j: (jnp.minimum(j, 2), 0, 0, 0)),
                  pl.BlockSpec((1, ATTN_W), lambda b, j: (0, 0))],
        out_specs=pl.BlockSpec((qb * BLK, ATTN_W), tile),
        compiler_params=_cparams(("arbitrary", "arbitrary")),
        name="swa_attention",
    )(sinks, q, kvx, kvx, kvx, jnp.asarray(table), gain)


def _ssd_kernel(xbc_ref, z_ref, dt_ref, cw_ref, cb_ref, dtb_ref, alog_ref, dsk_ref, nw_ref,
                y_ref, tail_ref, act_ref, state_ref, xd_ref, yd_ref, dout_ref):
    c = pl.program_id(1)

    @pl.when(c == 0)
    def _():
        tail_ref[...] = jnp.zeros_like(tail_ref)
        state_ref[...] = jnp.zeros_like(state_ref)

    valid = jnp.logical_or(c > 0, lax.broadcasted_iota(jnp.int32, (BLK, 1), 0) >= PAD)
    x = jnp.where(valid, xbc_ref[...].astype(F32), 0.0)
    tail_ref[SUBLANES:, :] = x
    acc = x * cw_ref[CONV_K - 1:CONV_K, :] + cb_ref[...]
    for s in range(1, CONV_K):
        acc = acc + tail_ref[SUBLANES - s:SUBLANES - s + BLK, :] * cw_ref[CONV_K - 1 - s:CONV_K - s, :]
    tail_ref[:SUBLANES, :] = x[BLK - SUBLANES:]
    act_ref[...] = acc * _sigmoid(acc)
    bm = act_ref[:, SSM_DI:SSM_DI + SSM_G * SSM_N]
    cm = act_ref[:, SSM_DI + SSM_G * SSM_N:]

    dtr = dt_ref[...] + dtb_ref[...]
    dt = jnp.maximum(dtr, 0.0) + jnp.log1p(jnp.exp(-jnp.abs(dtr)))
    dt = jnp.where(valid, dt, 0.0)
    da = dt * (-jnp.exp(alog_ref[...]))

    row = lax.broadcasted_iota(jnp.int32, (BLK, BLK), 0)
    col = lax.broadcasted_iota(jnp.int32, (BLK, BLK), 1)
    causal = row >= col
    tril = jnp.where(causal, 1.0, 0.0).astype(BF16)
    a_cs = sum(jnp.dot(tril, part, preferred_element_type=F32) for part in _split3(da))
    a_cs_t = a_cs.T
    d_state = jnp.exp(a_cs[BLK - 1:BLK, :] - a_cs)
    d_out = jnp.exp(a_cs)
    dtd = dt * d_state

    nt = (((1,), (1,)), ((), ()))
    cbs = [lax.dot_general(cm[:, SSM_N * g:SSM_N * (g + 1)].astype(BF16),
                           bm[:, SSM_N * g:SSM_N * (g + 1)].astype(BF16),
                           nt, preferred_element_type=F32) for g in range(SSM_G)]
    lo_half = col < SSM_P
    pairs = SSM_HEADS // 2
    for j in range(pairs):
        g = j // (pairs // SSM_G)
        h0, h1 = 2 * j, 2 * j + 1
        sl = slice(BLK * j, BLK * (j + 1))

        def per_head(v):
            return jnp.where(lo_half, v[:, h0:h0 + 1], v[:, h1:h1 + 1])

        xs_p = act_ref[:, sl]
        xdt = xs_p * per_head(dt)
        xd_ref[:, sl] = (xs_p * per_head(dtd)).astype(BF16)
        dout_ref[:, sl] = per_head(d_out)

        def intra(h):
            seg = a_cs[:, h:h + 1] - a_cs_t[h:h + 1, :]
            return (cbs[g] * jnp.exp(jnp.where(causal, seg, NEG))).astype(BF16)

        yd_ref[:, sl] = (
            jnp.dot(intra(h0), jnp.where(lo_half, xdt, 0.0).astype(BF16), preferred_element_type=F32)
            + jnp.dot(intra(h1), jnp.where(lo_half, 0.0, xdt).astype(BF16), preferred_element_type=F32))

    for g in range(SSM_G):
        gs = slice(SSM_GW * g, SSM_GW * (g + 1))
        st = state_ref[g]
        cm_g = cm[:, SSM_N * g:SSM_N * (g + 1)].astype(BF16)
        bm_t = bm[:, SSM_N * g:SSM_N * (g + 1)].T.astype(BF16)
        y_off = jnp.dot(cm_g, st.astype(BF16), preferred_element_type=F32) * dout_ref[:, gs]
        state_ref[g] = (st * dout_ref[BLK - 1:BLK, gs]
                        + jnp.dot(bm_t, xd_ref[:, gs], preferred_element_type=F32))
        y = yd_ref[:, gs] + y_off + act_ref[:, gs] * dsk_ref[:, gs]
        zz = z_ref[:, gs].astype(F32)
        y = y * (zz * _sigmoid(zz))
        y_ref[:, gs] = _rms(y, nw_ref[:, gs]).astype(y_ref.dtype)


def _ssd_call(xbc, z, dt_raw, conv_w, conv_b, dt_bias, a_log, d_skip, norm_w, bsz, nb):
    t = xbc.shape[0]
    blk = lambda b, n: (b * nb + n, 0)
    const = lambda b, n: (0, 0)
    return pl.pallas_call(
        _ssd_kernel,
        out_shape=jax.ShapeDtypeStruct((t, SSM_DI), BF16),
        grid=(bsz, nb),
        in_specs=[pl.BlockSpec((BLK, CONV_DIM), blk),
                  pl.BlockSpec((BLK, SSM_DI), blk),
                  pl.BlockSpec((BLK, BLK), blk),
                  pl.BlockSpec((CONV_K, CONV_DIM), const),
                  pl.BlockSpec((1, CONV_DIM), const),
                  pl.BlockSpec((1, BLK), const),
                  pl.BlockSpec((1, BLK), const),
                  pl.BlockSpec((1, SSM_DI), const),
                  pl.BlockSpec((1, SSM_DI), const)],
        out_specs=pl.BlockSpec((BLK, SSM_DI), blk),
        scratch_shapes=[pltpu.VMEM((SUBLANES + BLK, CONV_DIM), F32),
                        pltpu.VMEM((BLK, CONV_DIM), F32),
                        pltpu.VMEM((SSM_G, SSM_N, SSM_GW), F32),
                        pltpu.VMEM((BLK, SSM_DI), BF16),
                        pltpu.VMEM((BLK, SSM_DI), F32),
                        pltpu.VMEM((BLK, SSM_DI), F32)],
        compiler_params=_cparams(("arbitrary", "arbitrary")),
        name="conv_ssd",
    )(xbc, z, dt_raw, conv_w, conv_b, dt_bias, a_log, d_skip, norm_w)


def _outproj_kernel(ya_ref, ys_ref, h_ref, wa_ref, ws_ref, g_ref, hn_ref, u_ref):
    hn = (h_ref[...]
          + jnp.dot(ya_ref[...], wa_ref[...], preferred_element_type=F32)
          + jnp.dot(ys_ref[...], ws_ref[...], preferred_element_type=F32))
    hn_ref[...] = hn
    u_ref[...] = _rms(hn, g_ref[...]).astype(u_ref.dtype)


def _outproj_call(ya, ys, h, wa, ws, g, u_dtype):
    t = h.shape[0]
    tm = _pick_tile(t, 512)
    row = lambda i: (i, 0)
    const = lambda i: (0, 0)
    return pl.pallas_call(
        _outproj_kernel,
        out_shape=(jax.ShapeDtypeStruct((t, D_MODEL), F32),
                   jax.ShapeDtypeStruct((t, D_MODEL), u_dtype)),
        grid=(t // tm,),
        in_specs=[pl.BlockSpec((tm, ATTN_W), row),
                  pl.BlockSpec((tm, SSM_DI), row),
                  pl.BlockSpec((tm, D_MODEL), row),
                  pl.BlockSpec((ATTN_W, D_MODEL), const),
                  pl.BlockSpec((SSM_DI, D_MODEL), const),
                  pl.BlockSpec((1, D_MODEL), const)],
        out_specs=(pl.BlockSpec((tm, D_MODEL), row),
                   pl.BlockSpec((tm, D_MODEL), row)),
        compiler_params=_cparams(("arbitrary",)),
        name="out_proj",
    )(ya, ys, h, wa, ws, g)


def _swiglu_tile(x, w1_ref, w3_ref, w2_ref, act_ref):
    d_ff = w1_ref.shape[1]
    tf = _pick_tile(d_ff, FFN_TF)
    for c in range(d_ff // tf):
        sl = slice(tf * c, tf * (c + 1))
        a = jnp.dot(x, w1_ref[:, sl], preferred_element_type=F32)
        b = jnp.dot(x, w3_ref[:, sl], preferred_element_type=F32)
        act_ref[:, sl] = (a * _sigmoid(a) * b).astype(BF16)
    return jnp.dot(act_ref[...], w2_ref[...], preferred_element_type=F32)


def _ffn_kernel(u_ref, h_ref, w1_ref, w3_ref, w2_ref, g_ref, hn_ref, xn_ref, act_ref):
    hn = h_ref[...] + _swiglu_tile(u_ref[...], w1_ref, w3_ref, w2_ref, act_ref)
    hn_ref[...] = hn
    xn_ref[...] = _rms(hn, g_ref[...]).astype(xn_ref.dtype)


def _ffn_call(u, h, w1, w3, w2, g):
    t = h.shape[0]
    d_ff = w1.shape[1]
    tm = _pick_tile(t, 512)
    row = lambda i: (i, 0)
    const = lambda i: (0, 0)
    once = pl.Buffered(1)
    return pl.pallas_call(
        _ffn_kernel,
        out_shape=(jax.ShapeDtypeStruct((t, D_MODEL), F32),
                   jax.ShapeDtypeStruct((t, D_MODEL), BF16)),
        grid=(t // tm,),
        in_specs=[pl.BlockSpec((tm, D_MODEL), row),
                  pl.BlockSpec((tm, D_MODEL), row),
                  pl.BlockSpec((D_MODEL, d_ff), const, pipeline_mode=once),
                  pl.BlockSpec((D_MODEL, d_ff), const, pipeline_mode=once),
                  pl.BlockSpec((d_ff, D_MODEL), const, pipeline_mode=once),
                  pl.BlockSpec((1, D_MODEL), const)],
        out_specs=(pl.BlockSpec((tm, D_MODEL), row),
                   pl.BlockSpec((tm, D_MODEL), row)),
        scratch_shapes=[pltpu.VMEM((tm, d_ff), BF16)],
        compiler_params=_cparams(("arbitrary",)),
        name="dense_ffn",
    )(u, h, w1, w3, w2, g)


R_E1, R_E2, R_G1, R_G2, R_K1, R_K2 = range(6)


def _router_kernel(u_ref, rhi_ref, rlo_ref, route_ref, cnt_ref):
    tm = u_ref.shape[0]

    @pl.when(pl.program_id(0) == 0)
    def _():
        cnt_ref[...] = jnp.zeros_like(cnt_ref)

    u = u_ref[...]
    u_hi = u.astype(BF16)
    u_lo = (u - u_hi.astype(F32)).astype(BF16)
    logits = (jnp.dot(u_hi, rhi_ref[...], preferred_element_type=F32)
              + jnp.dot(u_lo, rhi_ref[...], preferred_element_type=F32)
              + jnp.dot(u_hi, rlo_ref[...], preferred_element_type=F32))
    lane = lax.broadcasted_iota(jnp.int32, (tm, BLK), 1)
    logits = jnp.where(lane < N_EXP, logits, -jnp.inf)
    m1 = jnp.max(logits, axis=-1, keepdims=True)
    i1 = jnp.min(jnp.where(logits == m1, lane, BLK), axis=-1, keepdims=True)
    rest = jnp.where(lane == i1, -jnp.inf, logits)
    m2 = jnp.max(rest, axis=-1, keepdims=True)
    i2 = jnp.min(jnp.where(rest == m2, lane, BLK), axis=-1, keepdims=True)
    e21 = jnp.exp(m2 - m1)
    g1 = 1.0 / (1.0 + e21)
    g2 = e21 / (1.0 + e21)

    sel1 = lane == i1
    sel2 = lane == i2
    onehot = jnp.where(sel1, 1.0, 0.0) + jnp.where(sel2, 1.0, 0.0)
    earlier = (lax.broadcasted_iota(jnp.int32, (tm, tm), 0)
               > lax.broadcasted_iota(jnp.int32, (tm, tm), 1))
    before = (jnp.dot(jnp.where(earlier, 1.0, 0.0).astype(BF16), onehot.astype(BF16),
                      preferred_element_type=F32) + cnt_ref[...])
    k1 = jnp.sum(jnp.where(sel1, before, 0.0), axis=-1, keepdims=True)
    k2 = jnp.sum(jnp.where(sel2, before, 0.0), axis=-1, keepdims=True)
    cnt_ref[...] += jnp.sum(onehot, axis=0, keepdims=True)

    rec = jnp.zeros((tm, BLK), F32)
    for idx, val in ((R_E1, i1.astype(F32)), (R_E2, i2.astype(F32)), (R_G1, g1), (R_G2, g2),
                     (R_K1, k1), (R_K2, k2)):
        rec = jnp.where(lane == idx, val, rec)
    route_ref[...] = rec


def _router_call(u, r_hi, r_lo):
    t = u.shape[0]
    tm = _pick_tile(t, 512)
    row = lambda i: (i, 0)
    const = lambda i: (0, 0)
    return pl.pallas_call(
        _router_kernel,
        out_shape=(jax.ShapeDtypeStruct((t, BLK), F32),
                   jax.ShapeDtypeStruct((1, BLK), F32)),
        grid=(t // tm,),
        in_specs=[pl.BlockSpec((tm, D_MODEL), row),
                  pl.BlockSpec((D_MODEL, BLK), const),
                  pl.BlockSpec((D_MODEL, BLK), const)],
        out_specs=(pl.BlockSpec((tm, BLK), row),
                   pl.BlockSpec((1, BLK), const)),
        compiler_params=_cparams(("arbitrary",)),
        name="router",
    )(u, r_hi, r_lo)


def _dispatch_kernel(zt_ref, pos_ref, u_ref, xs_hbm, zero_ref, sem, zsem, *, tm):
    n_fill = zt_ref.shape[0]

    def fill_copy(j):
        return pltpu.make_async_copy(zero_ref, xs_hbm.at[pl.ds(pl.multiple_of(zt_ref[j], MOE_TM), MOE_TM)], zsem)

    @pl.when(pl.program_id(0) == 0)
    def _():
        zero_ref[...] = jnp.zeros_like(zero_ref)
        for j in range(n_fill):
            fill_copy(j).start()
            fill_copy(j).wait()

    def row_copy(t, k):
        return pltpu.make_async_copy(u_ref.at[pl.ds(t, 1)], xs_hbm.at[pl.ds(pos_ref[2 * t + k], 1)], sem.at[k])

    def issue(t, carry):
        for k in range(2):
            row_copy(t, k).start(priority=k)
        return carry

    lax.fori_loop(0, tm, issue, 0, unroll=DMA_UNROLL)
    for k in range(2):
        pltpu.make_async_copy(u_ref, xs_hbm.at[pl.ds(0, tm)], sem.at[k]).wait()


def _dispatch_call(fill_tiles, pos_flat, u, r_max):
    t = u.shape[0]
    tm = _pick_tile(t, 512)
    grid_spec = pltpu.PrefetchScalarGridSpec(
        num_scalar_prefetch=1,
        grid=(t // tm,),
        in_specs=[pl.BlockSpec((2 * tm,), lambda i, zt: (i,), memory_space=pltpu.SMEM),
                  pl.BlockSpec((tm, D_MODEL), lambda i, zt: (i, 0))],
        out_specs=pl.BlockSpec(memory_space=pl.ANY),
        scratch_shapes=[pltpu.VMEM((MOE_TM, D_MODEL), F32),
                        pltpu.SemaphoreType.DMA((2,)),
                        pltpu.SemaphoreType.DMA(())],
    )
    return pl.pallas_call(
        functools.partial(_dispatch_kernel, tm=tm),
        out_shape=jax.ShapeDtypeStruct((r_max, D_MODEL), F32),
        grid_spec=grid_spec,
        compiler_params=pltpu.CompilerParams(dimension_semantics=("arbitrary",),
                                             vmem_limit_bytes=VMEM_LIMIT, has_side_effects=True),
        name="moe_dispatch",
    )(fill_tiles, pos_flat, u)


def _moe_kernel(te_ref, nv_ref, x_ref, w1_hbm, w3_hbm, w2_hbm, y_ref,
                w1_ref, w3_ref, w2_ref, up_stage, down_stage, act_ref, sem, *, layer):
    i = pl.program_id(0)
    e = te_ref[i]
    d_ff = w1_ref.shape[1]
    up_rows, down_rows = up_stage.shape[1], down_stage.shape[1]
    plan = ([(w1_hbm, w1_ref, up_stage, r) for r in range(0, D_MODEL, up_rows)]
            + [(w3_hbm, w3_ref, up_stage, r) for r in range(0, D_MODEL, up_rows)]
            + [(w2_hbm, w2_ref, down_stage, r) for r in range(0, d_ff, down_rows)])

    def chunk_copy(j):
        src, _, stage, r0 = plan[j]
        return pltpu.make_async_copy(src.at[layer, e, pl.ds(r0, stage.shape[1])], stage.at[j % 2], sem.at[j % 2])

    @pl.when(jnp.logical_or(i == 0, e != te_ref[jnp.maximum(i - 1, 0)]))
    def _():
        chunk_copy(0).start()
        for j in range(len(plan)):
            if j + 1 < len(plan):
                chunk_copy(j + 1).start()
            chunk_copy(j).wait()
            _, home, stage, r0 = plan[j]
            home[r0:r0 + stage.shape[1], :] = stage[j % 2].astype(BF16)

    @pl.when(i < nv_ref[0])
    def _():
        y_ref[...] = _swiglu_tile(x_ref[...].astype(BF16), w1_ref, w3_ref, w2_ref, act_ref)

    @pl.when(i >= nv_ref[0])
    def _():
        y_ref[...] = jnp.zeros_like(y_ref)


def _moe_call(tile_expert, n_valid, xs, w1, w3, w2, layer):
    r_max = xs.shape[0]
    d_ff = w1.shape[3]
    tm = MOE_TM
    grid_spec = pltpu.PrefetchScalarGridSpec(
        num_scalar_prefetch=2,
        grid=(r_max // tm,),
        in_specs=[pl.BlockSpec((tm, D_MODEL), lambda i, te, nv: (i, 0)),
                  pl.BlockSpec(memory_space=pl.ANY),
                  pl.BlockSpec(memory_space=pl.ANY),
                  pl.BlockSpec(memory_space=pl.ANY)],
        out_specs=pl.BlockSpec((tm, D_MODEL), lambda i, te, nv: (i, 0)),
        scratch_shapes=[pltpu.VMEM((D_MODEL, d_ff), BF16),
                        pltpu.VMEM((D_MODEL, d_ff), BF16),
                        pltpu.VMEM((d_ff, D_MODEL), BF16),
                        pltpu.VMEM((2, D_MODEL // WEIGHT_CHUNKS, d_ff), F32),
                        pltpu.VMEM((2, d_ff // WEIGHT_CHUNKS, D_MODEL), F32),
                        pltpu.VMEM((tm, d_ff), BF16),
                        pltpu.SemaphoreType.DMA((2,))],
    )
    return pl.pallas_call(
        functools.partial(_moe_kernel, layer=layer),
        out_shape=jax.ShapeDtypeStruct((r_max, D_MODEL), F32),
        grid_spec=grid_spec,
        compiler_params=pltpu.CompilerParams(dimension_semantics=("arbitrary",),
                                             vmem_limit_bytes=MOE_VMEM_LIMIT),
        name="moe_ffn",
    )(tile_expert, n_valid, xs, w1, w3, w2)


def _combine_kernel(pos_ref, h_ref, route_ref, g_ref, ys_hbm, *rest, tm, emit_h):
    if emit_h:
        hn_ref, xn_ref, y0_ref, y1_ref, sem = rest
    else:
        xn_ref, y0_ref, y1_ref, sem = rest
    bufs = (y0_ref, y1_ref)

    def row_copy(t, k):
        return pltpu.make_async_copy(ys_hbm.at[pl.ds(pos_ref[2 * t + k], 1)], bufs[k].at[pl.ds(t, 1)], sem.at[k])

    def issue(t, carry):
        for k in range(2):
            row_copy(t, k).start(priority=k)
        return carry

    lax.fori_loop(0, tm, issue, 0, unroll=DMA_UNROLL)
    for k in range(2):
        pltpu.make_async_copy(ys_hbm.at[pl.ds(0, tm)], bufs[k], sem.at[k]).wait()
    hn = (h_ref[...]
          + route_ref[:, R_G1:R_G1 + 1] * y0_ref[...]
          + route_ref[:, R_G2:R_G2 + 1] * y1_ref[...])
    if emit_h:
        hn_ref[...] = hn
    xn_ref[...] = _rms(hn, g_ref[...]).astype(xn_ref.dtype)


def _combine_call(pos_flat, h, route, g, ys, emit_h, bsz, nb):
    t = h.shape[0]
    if emit_h:
        tm = _pick_tile(t, 256)
        grid = (t // tm,)
        row = lambda i: (i, 0)
        flat = lambda i: (i,)
        const = lambda i: (0, 0)
        out_shape = (jax.ShapeDtypeStruct((t, D_MODEL), F32), jax.ShapeDtypeStruct((t, D_MODEL), BF16))
        out_specs = (pl.BlockSpec((tm, D_MODEL), row), pl.BlockSpec((tm, D_MODEL), row))
    else:
        tm = BLK
        grid = (bsz, nb)
        row = lambda b, n: (b * nb + n, 0)
        flat = lambda b, n: (b * nb + n,)
        const = lambda b, n: (0, 0)
        out_shape = jax.ShapeDtypeStruct((bsz * (nb - 1) * BLK, D_MODEL), F32)
        out_specs = pl.BlockSpec((tm, D_MODEL), lambda b, n: (b * (nb - 1) + jnp.maximum(n - 1, 0), 0))
    return pl.pallas_call(
        functools.partial(_combine_kernel, tm=tm, emit_h=emit_h),
        out_shape=out_shape,
        grid=grid,
        in_specs=[pl.BlockSpec((2 * tm,), flat, memory_space=pltpu.SMEM),
                  pl.BlockSpec((tm, D_MODEL), row),
                  pl.BlockSpec((tm, BLK), row),
                  pl.BlockSpec((1, D_MODEL), const),
                  pl.BlockSpec(memory_space=pl.ANY)],
        out_specs=out_specs,
        scratch_shapes=[pltpu.VMEM((tm, D_MODEL), F32),
                        pltpu.VMEM((tm, D_MODEL), F32),
                        pltpu.SemaphoreType.DMA((2,))],
        compiler_params=_cparams(("arbitrary",) * len(grid)),
        name="moe_combine",
    )(pos_flat, h, route, g, ys)


def _moe_layer(h, u, router, w1, w3, w2, layer, g_next, emit_h, bsz, nb):
    t = h.shape[0]
    tm = MOE_TM
    n_tiles = -(-(2 * t + N_EXP * (tm - 1)) // tm)
    r_max = n_tiles * tm

    r_pad = jnp.zeros((D_MODEL, BLK), F32).at[:, :N_EXP].set(router)
    r_hi = r_pad.astype(BF16)
    r_lo = (r_pad - r_hi.astype(F32)).astype(BF16)
    route, counts = _router_call(u, r_hi, r_lo)

    cnt = counts[0, :N_EXP].astype(jnp.int32)
    padded = (cnt + tm - 1) // tm * tm
    ends = jnp.cumsum(padded)
    starts = ends - padded
    experts = route[:, R_E1:R_E2 + 1].astype(jnp.int32)
    ranks = route[:, R_K1:R_K2 + 1].astype(jnp.int32)
    pos = (starts[experts] + ranks).reshape(-1)
    n_valid = (ends[N_EXP - 1] // tm).astype(jnp.int32).reshape(1)
    tile_start = jnp.minimum(jnp.arange(n_tiles, dtype=jnp.int32), n_valid[0] - 1) * tm
    tile_expert = jnp.sum(ends[None, :] <= tile_start[:, None], axis=1).astype(jnp.int32)
    idle = ends[N_EXP - 1] + jnp.arange(N_EXP, dtype=jnp.int32) * tm
    fill_tiles = jnp.clip(jnp.concatenate([ends - tm, idle]), 0, r_max - tm).astype(jnp.int32)

    xs = _dispatch_call(fill_tiles, pos, u, r_max)
    ys = _moe_call(tile_expert, n_valid, xs, w1, w3, w2, layer)
    return _combine_call(pos, h, route, g_next, ys, emit_h, bsz, nb)


def kernel(x, meta_tokens, norm_mix, w_in, conv_w, conv_b, dt_bias, a_log, d_skip, ssm_norm,
           attn_norm, sinks, w_out, norm_ffn, ffn_w1, ffn_w3, ffn_w2, router, moe_w1, moe_w3,
           moe_w2, final_norm):
    bsz, seq, _ = x.shape
    depth = w_in.shape[0]
    assert seq % BLK == 0
    nb = seq // BLK + 1
    lp = nb * BLK

    def row(v, width=None):
        v = v.astype(F32).reshape(1, -1)
        if width is not None:
            v = jnp.pad(v, ((0, 0), (0, width - v.shape[1])))
        return v

    h, xn = _embed_call(x, meta_tokens, row(norm_mix[0]))
    out = None
    for i in range(depth):
        w_main, w_dt = _pack_w_in(w_in[i])
        q, kvx, z, xbc, dt_raw = _inproj_call(xn, w_main, w_dt)
        y_attn = _attn_call(q, kvx, sinks[i].astype(F32), row(attn_norm[i]), bsz, nb)
        y_ssm = _ssd_call(xbc, z, dt_raw, conv_w[i].T.astype(F32), row(conv_b[i]),
                          row(dt_bias[i], BLK), row(a_log[i], BLK),
                          row(jnp.repeat(d_skip[i], SSM_P)), row(ssm_norm[i]), bsz, nb)
        wo = w_out[i].astype(BF16)
        h, u = _outproj_call(y_attn, y_ssm, h, wo[:ATTN_W], wo[ATTN_W:], row(norm_ffn[i]),
                             BF16 if i % 2 == 0 else F32)
        last = i == depth - 1
        g_next = row(final_norm) if last else row(norm_mix[i + 1])
        j = i // 2
        if i % 2 == 0:
            h, xn = _ffn_call(u, h, ffn_w1[j].astype(BF16), ffn_w3[j].astype(BF16),
                              ffn_w2[j].astype(BF16), g_next)
            if last:
                out = _norm_call(h, g_next, F32)
        else:
            res = _moe_layer(h, u, router[j], moe_w1.astype(F32), moe_w3.astype(F32),
                             moe_w2.astype(F32), j, g_next, not last, bsz, nb)
            if last:
                return res.reshape(bsz, seq, D_MODEL).astype(x.dtype)
            h, xn = res
    return out.reshape(bsz, lp, D_MODEL)[:, BLK:].astype(x.dtype)
```

```python
import functools

import numpy as np

import jax
import jax.numpy as jnp
from jax import lax
from jax.experimental import pallas as pl
from jax.experimental.pallas import tpu as pltpu

F32 = jnp.float32
BF16 = jnp.bfloat16

D_MODEL = 1024
N_META = 16
BLK = 128
SUBLANES = 8
PAD = BLK - N_META
ATTN_HEADS = 8
KV_HEADS = 2
HEAD_DIM = 64
ATTN_W = ATTN_HEADS * HEAD_DIM
KV_W = KV_HEADS * HEAD_DIM
SSM_HEADS = 24
SSM_P = 64
SSM_DI = SSM_HEADS * SSM_P
SSM_G = 2
SSM_N = 128
SSM_GW = SSM_DI // SSM_G
CONV_K = 4
CONV_DIM = SSM_DI + 2 * SSM_G * SSM_N
N_EXP = 8
EPS = 1e-6
NEG = -1e30

SRC_K = ATTN_W
SRC_V = ATTN_W + KV_W
SRC_Z = ATTN_W + 2 * KV_W
SRC_DT = SRC_Z + SSM_DI + CONV_DIM
KVX_W = 4 * KV_W
OFF_Q = 0
OFF_KVX = ATTN_W
OFF_Z = OFF_KVX + KVX_W
OFF_XBC = OFF_Z + SSM_DI
W_MAIN = OFF_XBC + CONV_DIM

ATTN_QB = 3
MOE_TM = 512
FFN_TF = 512
WEIGHT_CHUNKS = 8
DMA_UNROLL = 8
VMEM_LIMIT = 48 * 1024 * 1024
MOE_VMEM_LIMIT = 56 * 1024 * 1024


def _pick_tile(n, pref):
    t = pref
    while n % t:
        t //= 2
    return t


def _cparams(sem):
    return pltpu.CompilerParams(dimension_semantics=sem, vmem_limit_bytes=VMEM_LIMIT)


def _rms(xf, g):
    return xf * lax.rsqrt(jnp.mean(xf * xf, axis=-1, keepdims=True) + EPS) * g


def _sigmoid(x):
    return 1.0 / (1.0 + jnp.exp(-x))


def _split3(v):
    hi = v.astype(BF16)
    r1 = v - hi.astype(F32)
    mid = r1.astype(BF16)
    lo = (r1 - mid.astype(F32)).astype(BF16)
    return hi, mid, lo


def _norm_kernel(h_ref, g_ref, o_ref):
    o_ref[...] = _rms(h_ref[...], g_ref[...]).astype(o_ref.dtype)


def _norm_call(h, g, out_dtype=None):
    out_dtype = BF16 if out_dtype is None else out_dtype
    t = h.shape[0]
    tm = _pick_tile(t, 512)
    return pl.pallas_call(
        _norm_kernel,
        out_shape=jax.ShapeDtypeStruct((t, D_MODEL), out_dtype),
        grid=(t // tm,),
        in_specs=[pl.BlockSpec((tm, D_MODEL), lambda i: (i, 0)),
                  pl.BlockSpec((1, D_MODEL), lambda i: (0, 0))],
        out_specs=pl.BlockSpec((tm, D_MODEL), lambda i: (i, 0)),
        compiler_params=_cparams(("arbitrary",)),
        name="rmsnorm",
    )(h, g)


def _embed_kernel(x_ref, head_ref, g_ref, h_ref, xn_ref):
    h = jnp.where(pl.program_id(1) == 0, head_ref[...], x_ref[...].astype(F32))
    h_ref[...] = h
    xn_ref[...] = _rms(h, g_ref[...]).astype(xn_ref.dtype)


def _embed_call(x, meta_tokens, g):
    bsz, seq, _ = x.shape
    nb = seq // BLK + 1
    t = bsz * nb * BLK
    head = jnp.pad(meta_tokens.astype(F32), ((PAD, 0), (0, 0)))
    blk = lambda b, n: (b * nb + n, 0)
    const = lambda b, n: (0, 0)
    return pl.pallas_call(
        _embed_kernel,
        out_shape=(jax.ShapeDtypeStruct((t, D_MODEL), F32),
                   jax.ShapeDtypeStruct((t, D_MODEL), BF16)),
        grid=(bsz, nb),
        in_specs=[pl.BlockSpec((BLK, D_MODEL), lambda b, n: (b * (nb - 1) + jnp.maximum(n - 1, 0), 0)),
                  pl.BlockSpec((BLK, D_MODEL), const),
                  pl.BlockSpec((1, D_MODEL), const)],
        out_specs=(pl.BlockSpec((BLK, D_MODEL), blk),
                   pl.BlockSpec((BLK, D_MODEL), blk)),
        compiler_params=_cparams(("arbitrary", "arbitrary")),
        name="embed",
    )(x.reshape(bsz * seq, D_MODEL), head, g)


def _inproj_kernel(x_ref, w_ref, wdt_ref, q_ref, kvx_ref, z_ref, xbc_ref, dt_ref):
    x = x_ref[...]

    def mm(lo, n):
        return jnp.dot(x, w_ref[:, lo:lo + n], preferred_element_type=F32)

    q_ref[...] = (mm(OFF_Q, ATTN_W) * (HEAD_DIM ** -0.5)).astype(BF16)
    lane = lax.broadcasted_iota(jnp.int32, (1, KVX_W), 1)
    ones = jnp.where(jnp.logical_and(lane >= 2 * KV_W, (lane & HEAD_DIM) != 0), 1.0, 0.0)
    kvx_ref[...] = (mm(OFF_KVX, KVX_W) + ones).astype(BF16)
    for c in range(SSM_DI // 512):
        z_ref[:, 512 * c:512 * (c + 1)] = mm(OFF_Z + 512 * c, 512).astype(BF16)
    for c in range(CONV_DIM // 512):
        xbc_ref[:, 512 * c:512 * (c + 1)] = mm(OFF_XBC + 512 * c, 512).astype(BF16)
    dt_ref[...] = jnp.dot(x, wdt_ref[...], preferred_element_type=F32)


def _inproj_call(xn, w_main, w_dt):
    t = xn.shape[0]
    tm = _pick_tile(t, 512)
    row = lambda i: (i, 0)
    const = lambda i: (0, 0)
    return pl.pallas_call(
        _inproj_kernel,
        out_shape=(jax.ShapeDtypeStruct((t, ATTN_W), BF16),
                   jax.ShapeDtypeStruct((t, KVX_W), BF16),
                   jax.ShapeDtypeStruct((t, SSM_DI), BF16),
                   jax.ShapeDtypeStruct((t, CONV_DIM), BF16),
                   jax.ShapeDtypeStruct((t, BLK), F32)),
        grid=(t // tm,),
        in_specs=[pl.BlockSpec((tm, D_MODEL), row),
                  pl.BlockSpec((D_MODEL, W_MAIN), const),
                  pl.BlockSpec((D_MODEL, BLK), const)],
        out_specs=(pl.BlockSpec((tm, ATTN_W), row),
                   pl.BlockSpec((tm, KVX_W), row),
                   pl.BlockSpec((tm, SSM_DI), row),
                   pl.BlockSpec((tm, CONV_DIM), row),
                   pl.BlockSpec((tm, BLK), row)),
        compiler_params=_cparams(("arbitrary",)),
        name="in_proj",
    )(xn, w_main, w_dt)


def _pack_w_in(w):
    k0, k1 = w[:, SRC_K:SRC_K + HEAD_DIM], w[:, SRC_K + HEAD_DIM:SRC_V]
    v0, v1 = w[:, SRC_V:SRC_V + HEAD_DIM], w[:, SRC_V + HEAD_DIM:SRC_Z]
    zero = jnp.zeros_like(k0)
    w_main = jnp.concatenate([w[:, :SRC_K], k0, k0, k1, k1, v0, zero, v1, zero, w[:, SRC_Z:SRC_DT]], axis=1)
    w_dt = jnp.pad(w[:, SRC_DT:], ((0, 0), (0, BLK - SSM_HEADS)))
    return w_main.astype(BF16), w_dt.astype(BF16)


def _attn_bias():
    qi = np.arange(BLK)[:, None]
    c = np.arange(3 * BLK)[None, :]
    kb, ki = c // BLK, c % BLK
    out = []
    for n in range(3):
        qpos = n * BLK + qi - PAD
        vis_meta = (ki >= PAD) & (ki - PAD <= qpos)
        vis_prev = (ki > qi) & (n >= 2)
        vis_cur = (ki <= qi) & (n >= 1)
        vis = np.where(kb == 0, vis_meta, np.where(kb == 1, vis_prev, vis_cur))
        out.append(np.where(vis, 0.0, NEG))
    return np.stack(out).astype(np.float32)


def _attn_kernel(sink_ref, q_ref, kvc_ref, kvp_ref, kvm_ref, bias_ref, g_ref, o_ref):
    rep = ATTN_HEADS // KV_HEADS
    lo = lax.broadcasted_iota(jnp.int32, (BLK, BLK), 1) < HEAD_DIM
    srow = lax.broadcasted_iota(jnp.int32, (rep * BLK, 1), 0)
    nt = (((1,), (1,)), ((), ()))
    for i in range(q_ref.shape[0] // BLK):
        rows = slice(BLK * i, BLK * (i + 1))
        before = slice(BLK * (i - 1), BLK * i)
        bias = jnp.concatenate([bias_ref[i]] * rep, axis=0)
        pairs = []
        for g in range(KV_HEADS):
            ksl = slice(BLK * g, BLK * (g + 1))
            vsl = slice(2 * KV_W + BLK * g, 2 * KV_W + BLK * (g + 1))
            qs = []
            for a in range(rep // 2):
                two = q_ref[rows, rep * HEAD_DIM * g + BLK * a:rep * HEAD_DIM * g + BLK * (a + 1)]
                zero = jnp.zeros_like(two)
                qs += [jnp.where(lo, two, zero), jnp.where(lo, zero, two)]
            qs = jnp.concatenate(qs, axis=0)
            k_prev, v_prev = (kvp_ref[:, ksl], kvp_ref[:, vsl]) if i == 0 else (kvc_ref[before, ksl], kvc_ref[before, vsl])
            kcat = jnp.concatenate([kvm_ref[:, ksl], k_prev, kvc_ref[rows, ksl]], axis=0)
            vcat = jnp.concatenate([kvm_ref[:, vsl], v_prev, kvc_ref[rows, vsl]], axis=0)
            s = lax.dot_general(qs, kcat, nt, preferred_element_type=F32) + bias
            sink = sink_ref[rep * g + rep - 1]
            for a in range(rep - 2, -1, -1):
                sink = jnp.where(srow < BLK * (a + 1), sink_ref[rep * g + a], sink)
            m = jnp.maximum(jnp.max(s, axis=-1, keepdims=True), sink)
            p = jnp.exp(s - m).astype(BF16)
            o = jnp.dot(p, vcat, preferred_element_type=F32)
            y = o * pltpu.roll(1.0 / (o + jnp.exp(sink - m)), HEAD_DIM, 1)
            for a in range(rep // 2):
                even = y[2 * a * BLK:(2 * a + 1) * BLK]
                odd = y[(2 * a + 1) * BLK:(2 * a + 2) * BLK]
                pairs.append(jnp.where(lo, even, pltpu.roll(odd, HEAD_DIM, 1)))
        o_ref[rows, :] = _rms(jnp.concatenate(pairs, axis=-1), g_ref[...]).astype(o_ref.dtype)


def _attn_call(q, kvx, sinks, gain, bsz, nb):
    t = q.shape[0]
    qb = ATTN_QB if nb % ATTN_QB == 0 else 1
    steps = nb // qb
    masks = _attn_bias()
    table = np.stack([masks[[min(v * qb + i, 2) for i in range(qb)]] for v in range(3)])
    tile = lambda b, j: (b * steps + j, 0)
    return pl.pallas_call(
        _attn_kernel,
        out_shape=jax.ShapeDtypeStruct((t, ATTN_W), BF16),
        grid=(bsz, steps),
        in_specs=[pl.BlockSpec(memory_space=pltpu.SMEM),
                  pl.BlockSpec((qb * BLK, ATTN_W), tile),
                  pl.BlockSpec((qb * BLK, KVX_W), tile),
                  pl.BlockSpec((BLK, KVX_W), lambda b, j: (b * nb + jnp.maximum(j * qb - 1, 0), 0)),
                  pl.BlockSpec((BLK, KVX_W), lambda b, j: (b * nb, 0)),
                  pl.BlockSpec((None, qb, BLK, 3 * BLK), lambda b, j: (jnp.minimum(j, 2), 0, 0, 0)),
                  pl.BlockSpec((1, ATTN_W), lambda b, j: (0, 0))],
        out_specs=pl.BlockSpec((qb * BLK, ATTN_W), tile),
        compiler_params=_cparams(("arbitrary", "arbitrary")),
        name="swa_attention",
    )(sinks, q, kvx, kvx, kvx, jnp.asarray(table), gain)


def _ssd_kernel(xbc_ref, z_ref, dt_ref, cw_ref, cb_ref, dtb_ref, alog_ref, dsk_ref, nw_ref,
                y_ref, tail_ref, act_ref, state_ref, xd_ref, yd_ref, dout_ref):
    c = pl.program_id(1)

    @pl.when(c == 0)
    def _():
        tail_ref[...] = jnp.zeros_like(tail_ref)
        state_ref[...] = jnp.zeros_like(state_ref)

    valid = jnp.logical_or(c > 0, lax.broadcasted_iota(jnp.int32, (BLK, 1), 0) >= PAD)
    x = jnp.where(valid, xbc_ref[...].astype(F32), 0.0)
    tail_ref[SUBLANES:, :] = x
    acc = x * cw_ref[CONV_K - 1:CONV_K, :] + cb_ref[...]
    for s in range(1, CONV_K):
        acc = acc + tail_ref[SUBLANES - s:SUBLANES - s + BLK, :] * cw_ref[CONV_K - 1 - s:CONV_K - s, :]
    tail_ref[:SUBLANES, :] = x[BLK - SUBLANES:]
    act_ref[...] = acc * _sigmoid(acc)
    bm = act_ref[:, SSM_DI:SSM_DI + SSM_G * SSM_N]
    cm = act_ref[:, SSM_DI + SSM_G * SSM_N:]

    dtr = dt_ref[...] + dtb_ref[...]
    dt = jnp.maximum(dtr, 0.0) + jnp.log1p(jnp.exp(-jnp.abs(dtr)))
    dt = jnp.where(valid, dt, 0.0)
    da = dt * (-jnp.exp(alog_ref[...]))

    row = lax.broadcasted_iota(jnp.int32, (BLK, BLK), 0)
    col = lax.broadcasted_iota(jnp.int32, (BLK, BLK), 1)
    causal = row >= col
    tril = jnp.where(causal, 1.0, 0.0).astype(BF16)
    a_cs = sum(jnp.dot(tril, part, preferred_element_type=F32) for part in _split3(da))
    a_cs_t = a_cs.T
    d_state = jnp.exp(a_cs[BLK - 1:BLK, :] - a_cs)
    d_out = jnp.exp(a_cs)
    dtd = dt * d_state

    nt = (((1,), (1,)), ((), ()))
    cbs = [lax.dot_general(cm[:, SSM_N * g:SSM_N * (g + 1)].astype(BF16),
                           bm[:, SSM_N * g:SSM_N * (g + 1)].astype(BF16),
                           nt, preferred_element_type=F32) for g in range(SSM_G)]
    lo_half = col < SSM_P
    pairs = SSM_HEADS // 2
    for j in range(pairs):
        g = j // (pairs // SSM_G)
        h0, h1 = 2 * j, 2 * j + 1
        sl = slice(BLK * j, BLK * (j + 1))

        def per_head(v):
            return jnp.where(lo_half, v[:, h0:h0 + 1], v[:, h1:h1 + 1])

        xs_p = act_ref[:, sl]
        xdt = xs_p * per_head(dt)
        xd_ref[:, sl] = (xs_p * per_head(dtd)).astype(BF16)
        dout_ref[:, sl] = per_head(d_out)

        def intra(h):
            seg = a_cs[:, h:h + 1] - a_cs_t[h:h + 1, :]
            return (cbs[g] * jnp.exp(jnp.where(causal, seg, NEG))).astype(BF16)

        yd_ref[:, sl] = (
            jnp.dot(intra(h0), jnp.where(lo_half, xdt, 0.0).astype(BF16), preferred_element_type=F32)
            + jnp.dot(intra(h1), jnp.where(lo_half, 0.0, xdt).astype(BF16), preferred_element_type=F32))

    for g in range(SSM_G):
        gs = slice(SSM_GW * g, SSM_GW * (g + 1))
        st = state_ref[g]
        cm_g = cm[:, SSM_N * g:SSM_N * (g + 1)].astype(BF16)
        bm_t = bm[:, SSM_N * g:SSM_N * (g + 1)].T.astype(BF16)
        y_off = jnp.dot(cm_g, st.astype(BF16), preferred_element_type=F32) * dout_ref[:, gs]
        state_ref[g] = (st * dout_ref[BLK - 1:BLK, gs]
                        + jnp.dot(bm_t, xd_ref[:, gs], preferred_element_type=F32))
        y = yd_ref[:, gs] + y_off + act_ref[:, gs] * dsk_ref[:, gs]
        zz = z_ref[:, gs].astype(F32)
        y = y * (zz * _sigmoid(zz))
        y_ref[:, gs] = _rms(y, nw_ref[:, gs]).astype(y_ref.dtype)


def _ssd_call(xbc, z, dt_raw, conv_w, conv_b, dt_bias, a_log, d_skip, norm_w, bsz, nb):
    t = xbc.shape[0]
    blk = lambda b, n: (b * nb + n, 0)
    const = lambda b, n: (0, 0)
    return pl.pallas_call(
        _ssd_kernel,
        out_shape=jax.ShapeDtypeStruct((t, SSM_DI), BF16),
        grid=(bsz, nb),
        in_specs=[pl.BlockSpec((BLK, CONV_DIM), blk),
                  pl.BlockSpec((BLK, SSM_DI), blk),
                  pl.BlockSpec((BLK, BLK), blk),
                  pl.BlockSpec((CONV_K, CONV_DIM), const),
                  pl.BlockSpec((1, CONV_DIM), const),
                  pl.BlockSpec((1, BLK), const),
                  pl.BlockSpec((1, BLK), const),
                  pl.BlockSpec((1, SSM_DI), const),
                  pl.BlockSpec((1, SSM_DI), const)],
        out_specs=pl.BlockSpec((BLK, SSM_DI), blk),
        scratch_shapes=[pltpu.VMEM((SUBLANES + BLK, CONV_DIM), F32),
                        pltpu.VMEM((BLK, CONV_DIM), F32),
                        pltpu.VMEM((SSM_G, SSM_N, SSM_GW), F32),
                        pltpu.VMEM((BLK, SSM_DI), BF16),
                        pltpu.VMEM((BLK, SSM_DI), F32),
                        pltpu.VMEM((BLK, SSM_DI), F32)],
        compiler_params=_cparams(("arbitrary", "arbitrary")),
        name="conv_ssd",
    )(xbc, z, dt_raw, conv_w, conv_b, dt_bias, a_log, d_skip, norm_w)


def _outproj_kernel(ya_ref, ys_ref, h_ref, wa_ref, ws_ref, g_ref, hn_ref, u_ref):
    hn = (h_ref[...]
          + jnp.dot(ya_ref[...], wa_ref[...], preferred_element_type=F32)
          + jnp.dot(ys_ref[...], ws_ref[...], preferred_element_type=F32))
    hn_ref[...] = hn
    u_ref[...] = _rms(hn, g_ref[...]).astype(u_ref.dtype)


def _outproj_call(ya, ys, h, wa, ws, g, u_dtype):
    t = h.shape[0]
    tm = _pick_tile(t, 512)
    row = lambda i: (i, 0)
    const = lambda i: (0, 0)
    return pl.pallas_call(
        _outproj_kernel,
        out_shape=(jax.ShapeDtypeStruct((t, D_MODEL), F32),
                   jax.ShapeDtypeStruct((t, D_MODEL), u_dtype)),
        grid=(t // tm,),
        in_specs=[pl.BlockSpec((tm, ATTN_W), row),
                  pl.BlockSpec((tm, SSM_DI), row),
                  pl.BlockSpec((tm, D_MODEL), row),
                  pl.BlockSpec((ATTN_W, D_MODEL), const),
                  pl.BlockSpec((SSM_DI, D_MODEL), const),
                  pl.BlockSpec((1, D_MODEL), const)],
        out_specs=(pl.BlockSpec((tm, D_MODEL), row),
                   pl.BlockSpec((tm, D_MODEL), row)),
        compiler_params=_cparams(("arbitrary",)),
        name="out_proj",
    )(ya, ys, h, wa, ws, g)


def _swiglu_tile(x, w1_ref, w3_ref, w2_ref, act_ref):
    d_ff = w1_ref.shape[1]
    tf = _pick_tile(d_ff, FFN_TF)
    for c in range(d_ff // tf):
        sl = slice(tf * c, tf * (c + 1))
        a = jnp.dot(x, w1_ref[:, sl], preferred_element_type=F32)
        b = jnp.dot(x, w3_ref[:, sl], preferred_element_type=F32)
        act_ref[:, sl] = (a * _sigmoid(a) * b).astype(BF16)
    return jnp.dot(act_ref[...], w2_ref[...], preferred_element_type=F32)


def _ffn_kernel(u_ref, h_ref, w1_ref, w3_ref, w2_ref, g_ref, hn_ref, xn_ref, act_ref):
    hn = h_ref[...] + _swiglu_tile(u_ref[...], w1_ref, w3_ref, w2_ref, act_ref)
    hn_ref[...] = hn
    xn_ref[...] = _rms(hn, g_ref[...]).astype(xn_ref.dtype)


def _ffn_call(u, h, w1, w3, w2, g):
    t = h.shape[0]
    d_ff = w1.shape[1]
    tm = _pick_tile(t, 512)
    row = lambda i: (i, 0)
    const = lambda i: (0, 0)
    once = pl.Buffered(1)
    return pl.pallas_call(
        _ffn_kernel,
        out_shape=(jax.ShapeDtypeStruct((t, D_MODEL), F32),
                   jax.ShapeDtypeStruct((t, D_MODEL), BF16)),
        grid=(t // tm,),
        in_specs=[pl.BlockSpec((tm, D_MODEL), row),
                  pl.BlockSpec((tm, D_MODEL), row),
                  pl.BlockSpec((D_MODEL, d_ff), const, pipeline_mode=once),
                  pl.BlockSpec((D_MODEL, d_ff), const, pipeline_mode=once),
                  pl.BlockSpec((d_ff, D_MODEL), const, pipeline_mode=once),
                  pl.BlockSpec((1, D_MODEL), const)],
        out_specs=(pl.BlockSpec((tm, D_MODEL), row),
                   pl.BlockSpec((tm, D_MODEL), row)),
        scratch_shapes=[pltpu.VMEM((tm, d_ff), BF16)],
        compiler_params=_cparams(("arbitrary",)),
        name="dense_ffn",
    )(u, h, w1, w3, w2, g)


R_E1, R_E2, R_G1, R_G2, R_K1, R_K2 = range(6)


def _router_kernel(u_ref, rhi_ref, rlo_ref, route_ref, cnt_ref):
    tm = u_ref.shape[0]

    @pl.when(pl.program_id(0) == 0)
    def _():
        cnt_ref[...] = jnp.zeros_like(cnt_ref)

    u = u_ref[...]
    u_hi = u.astype(BF16)
    u_lo = (u - u_hi.astype(F32)).astype(BF16)
    logits = (jnp.dot(u_hi, rhi_ref[...], preferred_element_type=F32)
              + jnp.dot(u_lo, rhi_ref[...], preferred_element_type=F32)
              + jnp.dot(u_hi, rlo_ref[...], preferred_element_type=F32))
    lane = lax.broadcasted_iota(jnp.int32, (tm, BLK), 1)
    logits = jnp.where(lane < N_EXP, logits, -jnp.inf)
    m1 = jnp.max(logits, axis=-1, keepdims=True)
    i1 = jnp.min(jnp.where(logits == m1, lane, BLK), axis=-1, keepdims=True)
    rest = jnp.where(lane == i1, -jnp.inf, logits)
    m2 = jnp.max(rest, axis=-1, keepdims=True)
    i2 = jnp.min(jnp.where(rest == m2, lane, BLK), axis=-1, keepdims=True)
    e21 = jnp.exp(m2 - m1)
    g1 = 1.0 / (1.0 + e21)
    g2 = e21 / (1.0 + e21)

    sel1 = lane == i1
    sel2 = lane == i2
    onehot = jnp.where(sel1, 1.0, 0.0) + jnp.where(sel2, 1.0, 0.0)
    earlier = (lax.broadcasted_iota(jnp.int32, (tm, tm), 0)
               > lax.broadcasted_iota(jnp.int32, (tm, tm), 1))
    before = (jnp.dot(jnp.where(earlier, 1.0, 0.0).astype(BF16), onehot.astype(BF16),
                      preferred_element_type=F32) + cnt_ref[...])
    k1 = jnp.sum(jnp.where(sel1, before, 0.0), axis=-1, keepdims=True)
    k2 = jnp.sum(jnp.where(sel2, before, 0.0), axis=-1, keepdims=True)
    cnt_ref[...] += jnp.sum(onehot, axis=0, keepdims=True)

    rec = jnp.zeros((tm, BLK), F32)
    for idx, val in ((R_E1, i1.astype(F32)), (R_E2, i2.astype(F32)), (R_G1, g1), (R_G2, g2),
                     (R_K1, k1), (R_K2, k2)):
        rec = jnp.where(lane == idx, val, rec)
    route_ref[...] = rec


def _router_call(u, r_hi, r_lo):
    t = u.shape[0]
    tm = _pick_tile(t, 512)
    row = lambda i: (i, 0)
    const = lambda i: (0, 0)
    return pl.pallas_call(
        _router_kernel,
        out_shape=(jax.ShapeDtypeStruct((t, BLK), F32),
                   jax.ShapeDtypeStruct((1, BLK), F32)),
        grid=(t // tm,),
        in_specs=[pl.BlockSpec((tm, D_MODEL), row),
                  pl.BlockSpec((D_MODEL, BLK), const),
                  pl.BlockSpec((D_MODEL, BLK), const)],
        out_specs=(pl.BlockSpec((tm, BLK), row),
                   pl.BlockSpec((1, BLK), const)),
        compiler_params=_cparams(("arbitrary",)),
        name="router",
    )(u, r_hi, r_lo)


def _dispatch_kernel(zt_ref, pos_ref, u_ref, xs_hbm, zero_ref, stage_ref, sem, zsem, *, tm):
    n_fill = zt_ref.shape[0]
    step = pl.program_id(0)
    slot = lax.rem(step, 2)

    def fill_copy(j):
        return pltpu.make_async_copy(zero_ref, xs_hbm.at[pl.ds(pl.multiple_of(zt_ref[j], MOE_TM), MOE_TM)], zsem)

    @pl.when(pl.program_id(0) == 0)
    def _():
        zero_ref[...] = jnp.zeros_like(zero_ref)
        for j in range(n_fill):
            fill_copy(j).start()
            fill_copy(j).wait()

    stage_ref[slot] = u_ref[...]

    def issue(t, carry):
        for k in range(2):
            pltpu.make_async_copy(stage_ref.at[slot, pl.ds(t, 1)],
                                  xs_hbm.at[pl.ds(pos_ref[2 * t + k], 1)], sem.at[slot, k]).start(priority=k)
        return carry

    lax.fori_loop(0, tm, issue, 0, unroll=DMA_UNROLL)

    def drain(s):
        for k in range(2):
            pltpu.make_async_copy(stage_ref.at[s], xs_hbm.at[pl.ds(0, tm)], sem.at[s, k]).wait()

    @pl.when(step > 0)
    def _():
        drain(1 - slot)

    @pl.when(step == pl.num_programs(0) - 1)
    def _():
        drain(slot)


def _dispatch_call(fill_tiles, pos_flat, u, r_max):
    t = u.shape[0]
    tm = _pick_tile(t, 512)
    grid_spec = pltpu.PrefetchScalarGridSpec(
        num_scalar_prefetch=1,
        grid=(t // tm,),
        in_specs=[pl.BlockSpec((2 * tm,), lambda i, zt: (i,), memory_space=pltpu.SMEM),
                  pl.BlockSpec((tm, D_MODEL), lambda i, zt: (i, 0))],
        out_specs=pl.BlockSpec(memory_space=pl.ANY),
        scratch_shapes=[pltpu.VMEM((MOE_TM, D_MODEL), F32),
                        pltpu.VMEM((2, tm, D_MODEL), F32),
                        pltpu.SemaphoreType.DMA((2, 2)),
                        pltpu.SemaphoreType.DMA(())],
    )
    return pl.pallas_call(
        functools.partial(_dispatch_kernel, tm=tm),
        out_shape=jax.ShapeDtypeStruct((r_max, D_MODEL), F32),
        grid_spec=grid_spec,
        compiler_params=pltpu.CompilerParams(dimension_semantics=("arbitrary",),
                                             vmem_limit_bytes=VMEM_LIMIT, has_side_effects=True),
        name="moe_dispatch",
    )(fill_tiles, pos_flat, u)


def _moe_kernel(te_ref, nv_ref, x_ref, w1_hbm, w3_hbm, w2_hbm, y_ref,
                w1_ref, w3_ref, w2_ref, up_stage, down_stage, act_ref, sem, *, layer):
    i = pl.program_id(0)
    e = te_ref[i]
    d_ff = w1_ref.shape[1]
    up_rows, down_rows = up_stage.shape[1], down_stage.shape[1]
    plan = ([(w1_hbm, w1_ref, up_stage, r) for r in range(0, D_MODEL, up_rows)]
            + [(w3_hbm, w3_ref, up_stage, r) for r in range(0, D_MODEL, up_rows)]
            + [(w2_hbm, w2_ref, down_stage, r) for r in range(0, d_ff, down_rows)])

    def chunk_copy(j):
        src, _, stage, r0 = plan[j]
        return pltpu.make_async_copy(src.at[layer, e, pl.ds(r0, stage.shape[1])], stage.at[j % 2], sem.at[j % 2])

    @pl.when(jnp.logical_or(i == 0, e != te_ref[jnp.maximum(i - 1, 0)]))
    def _():
        chunk_copy(0).start()
        for j in range(len(plan)):
            if j + 1 < len(plan):
                chunk_copy(j + 1).start()
            chunk_copy(j).wait()
            _, home, stage, r0 = plan[j]
            home[r0:r0 + stage.shape[1], :] = stage[j % 2].astype(BF16)

    @pl.when(i < nv_ref[0])
    def _():
        y_ref[...] = _swiglu_tile(x_ref[...].astype(BF16), w1_ref, w3_ref, w2_ref, act_ref)

    @pl.when(i >= nv_ref[0])
    def _():
        y_ref[...] = jnp.zeros_like(y_ref)


def _moe_call(tile_expert, n_valid, xs, w1, w3, w2, layer):
    r_max = xs.shape[0]
    d_ff = w1.shape[3]
    tm = MOE_TM
    grid_spec = pltpu.PrefetchScalarGridSpec(
        num_scalar_prefetch=2,
        grid=(r_max // tm,),
        in_specs=[pl.BlockSpec((tm, D_MODEL), lambda i, te, nv: (i, 0)),
                  pl.BlockSpec(memory_space=pl.ANY),
                  pl.BlockSpec(memory_space=pl.ANY),
                  pl.BlockSpec(memory_space=pl.ANY)],
        out_specs=pl.BlockSpec((tm, D_MODEL), lambda i, te, nv: (i, 0)),
        scratch_shapes=[pltpu.VMEM((D_MODEL, d_ff), BF16),
                        pltpu.VMEM((D_MODEL, d_ff), BF16),
                        pltpu.VMEM((d_ff, D_MODEL), BF16),
                        pltpu.VMEM((2, D_MODEL // WEIGHT_CHUNKS, d_ff), F32),
                        pltpu.VMEM((2, d_ff // WEIGHT_CHUNKS, D_MODEL), F32),
                        pltpu.VMEM((tm, d_ff), BF16),
                        pltpu.SemaphoreType.DMA((2,))],
    )
    return pl.pallas_call(
        functools.partial(_moe_kernel, layer=layer),
        out_shape=jax.ShapeDtypeStruct((r_max, D_MODEL), F32),
        grid_spec=grid_spec,
        compiler_params=pltpu.CompilerParams(dimension_semantics=("arbitrary",),
                                             vmem_limit_bytes=MOE_VMEM_LIMIT),
        name="moe_ffn",
    )(tile_expert, n_valid, xs, w1, w3, w2)


def _combine_kernel(pos_ref, pos_next_ref, h_ref, route_ref, g_ref, ys_hbm, *rest, tm, emit_h):
    if emit_h:
        hn_ref, xn_ref, y_ref, sem = rest
    else:
        xn_ref, y_ref, sem = rest
    step = pl.program_id(0)
    n_steps = pl.num_programs(0)
    if not emit_h:
        step = step * pl.num_programs(1) + pl.program_id(1)
        n_steps = n_steps * pl.num_programs(1)
    slot = lax.rem(step, 2)

    def gather(table_ref, into):
        def issue(t, carry):
            for k in range(2):
                pltpu.make_async_copy(ys_hbm.at[pl.ds(table_ref[2 * t + k], 1)],
                                      y_ref.at[into, k, pl.ds(t, 1)], sem.at[into, k]).start(priority=k)
            return carry
        lax.fori_loop(0, tm, issue, 0, unroll=DMA_UNROLL)

    @pl.when(step == 0)
    def _():
        gather(pos_ref, 0)

    @pl.when(step + 1 < n_steps)
    def _():
        gather(pos_next_ref, 1 - slot)

    for k in range(2):
        pltpu.make_async_copy(ys_hbm.at[pl.ds(0, tm)], y_ref.at[slot, k], sem.at[slot, k]).wait()
    hn = (h_ref[...]
          + route_ref[:, R_G1:R_G1 + 1] * y_ref[slot, 0]
          + route_ref[:, R_G2:R_G2 + 1] * y_ref[slot, 1])
    if emit_h:
        hn_ref[...] = hn
    xn_ref[...] = _rms(hn, g_ref[...]).astype(xn_ref.dtype)


def _combine_call(pos_flat, h, route, g, ys, emit_h, bsz, nb):
    t = h.shape[0]
    if emit_h:
        tm = _pick_tile(t, 256)
        grid = (t // tm,)
        row = lambda i: (i, 0)
        flat = lambda i: (i,)
        flat_next = lambda i: (jnp.minimum(i + 1, t // tm - 1),)
        const = lambda i: (0, 0)
        out_shape = (jax.ShapeDtypeStruct((t, D_MODEL), F32), jax.ShapeDtypeStruct((t, D_MODEL), BF16))
        out_specs = (pl.BlockSpec((tm, D_MODEL), row), pl.BlockSpec((tm, D_MODEL), row))
    else:
        tm = BLK
        grid = (bsz, nb)
        row = lambda b, n: (b * nb + n, 0)
        flat = lambda b, n: (b * nb + n,)
        flat_next = lambda b, n: (jnp.minimum(b * nb + n + 1, bsz * nb - 1),)
        const = lambda b, n: (0, 0)
        out_shape = jax.ShapeDtypeStruct((bsz * (nb - 1) * BLK, D_MODEL), F32)
        out_specs = pl.BlockSpec((tm, D_MODEL), lambda b, n: (b * (nb - 1) + jnp.maximum(n - 1, 0), 0))
    return pl.pallas_call(
        functools.partial(_combine_kernel, tm=tm, emit_h=emit_h),
        out_shape=out_shape,
        grid=grid,
        in_specs=[pl.BlockSpec((2 * tm,), flat, memory_space=pltpu.SMEM),
                  pl.BlockSpec((2 * tm,), flat_next, memory_space=pltpu.SMEM),
                  pl.BlockSpec((tm, D_MODEL), row),
                  pl.BlockSpec((tm, BLK), row),
                  pl.BlockSpec((1, D_MODEL), const),
                  pl.BlockSpec(memory_space=pl.ANY)],
        out_specs=out_specs,
        scratch_shapes=[pltpu.VMEM((2, 2, tm, D_MODEL), F32),
                        pltpu.SemaphoreType.DMA((2, 2))],
        compiler_params=_cparams(("arbitrary",) * len(grid)),
        name="moe_combine",
    )(pos_flat, pos_flat, h, route, g, ys)


def _moe_layer(h, u, router, w1, w3, w2, layer, g_next, emit_h, bsz, nb):
    t = h.shape[0]
    tm = MOE_TM
    n_tiles = -(-(2 * t + N_EXP * (tm - 1)) // tm)
    r_max = n_tiles * tm

    r_pad = jnp.zeros((D_MODEL, BLK), F32).at[:, :N_EXP].set(router)
    r_hi = r_pad.astype(BF16)
    r_lo = (r_pad - r_hi.astype(F32)).astype(BF16)
    route, counts = _router_call(u, r_hi, r_lo)

    cnt = counts[0, :N_EXP].astype(jnp.int32)
    padded = (cnt + tm - 1) // tm * tm
    ends = jnp.cumsum(padded)
    starts = ends - padded
    experts = route[:, R_E1:R_E2 + 1].astype(jnp.int32)
    ranks = route[:, R_K1:R_K2 + 1].astype(jnp.int32)
    pos = (starts[experts] + ranks).reshape(-1)
    n_valid = (ends[N_EXP - 1] // tm).astype(jnp.int32).reshape(1)
    tile_start = jnp.minimum(jnp.arange(n_tiles, dtype=jnp.int32), n_valid[0] - 1) * tm
    tile_expert = jnp.sum(ends[None, :] <= tile_start[:, None], axis=1).astype(jnp.int32)
    idle = ends[N_EXP - 1] + jnp.arange(N_EXP, dtype=jnp.int32) * tm
    fill_tiles = jnp.clip(jnp.concatenate([ends - tm, idle]), 0, r_max - tm).astype(jnp.int32)

    xs = _dispatch_call(fill_tiles, pos, u, r_max)
    ys = _moe_call(tile_expert, n_valid, xs, w1, w3, w2, layer)
    return _combine_call(pos, h, route, g_next, ys, emit_h, bsz, nb)


def kernel(x, meta_tokens, norm_mix, w_in, conv_w, conv_b, dt_bias, a_log, d_skip, ssm_norm,
           attn_norm, sinks, w_out, norm_ffn, ffn_w1, ffn_w3, ffn_w2, router, moe_w1, moe_w3,
           moe_w2, final_norm):
    bsz, seq, _ = x.shape
    depth = w_in.shape[0]
    assert seq % BLK == 0
    nb = seq // BLK + 1
    lp = nb * BLK

    def row(v, width=None):
        v = v.astype(F32).reshape(1, -1)
        if width is not None:
            v = jnp.pad(v, ((0, 0), (0, width - v.shape[1])))
        return v

    h, xn = _embed_call(x, meta_tokens, row(norm_mix[0]))
    out = None
    for i in range(depth):
        w_main, w_dt = _pack_w_in(w_in[i])
        q, kvx, z, xbc, dt_raw = _inproj_call(xn, w_main, w_dt)
        y_attn = _attn_call(q, kvx, sinks[i].astype(F32), row(attn_norm[i]), bsz, nb)
        y_ssm = _ssd_call(xbc, z, dt_raw, conv_w[i].T.astype(F32), row(conv_b[i]),
                          row(dt_bias[i], BLK), row(a_log[i], BLK),
                          row(jnp.repeat(d_skip[i], SSM_P)), row(ssm_norm[i]), bsz, nb)
        wo = w_out[i].astype(BF16)
        h, u = _outproj_call(y_attn, y_ssm, h, wo[:ATTN_W], wo[ATTN_W:], row(norm_ffn[i]),
                             BF16 if i % 2 == 0 else F32)
        last = i == depth - 1
        g_next = row(final_norm) if last else row(norm_mix[i + 1])
        j = i // 2
        if i % 2 == 0:
            h, xn = _ffn_call(u, h, ffn_w1[j].astype(BF16), ffn_w3[j].astype(BF16),
                              ffn_w2[j].astype(BF16), g_next)
            if last:
                out = _norm_call(h, g_next, F32)
        else:
            res = _moe_layer(h, u, router[j], moe_w1.astype(F32), moe_w3.astype(F32),
                             moe_w2.astype(F32), j, g_next, not last, bsz, nb)
            if last:
                return res.reshape(bsz, seq, D_MODEL).astype(x.dtype)
            h, xn = res
    return out.reshape(bsz, lp, D_MODEL)[:, BLK:].astype(x.dtype)
```

```python
import functools

import numpy as np

import jax
import jax.numpy as jnp
from jax import lax
from jax.experimental import pallas as pl
from jax.experimental.pallas import tpu as pltpu

F32 = jnp.float32
BF16 = jnp.bfloat16

D_MODEL = 1024
N_META = 16
BLK = 128
SUBLANES = 8
PAD = BLK - N_META
ATTN_HEADS = 8
KV_HEADS = 2
HEAD_DIM = 64
ATTN_W = ATTN_HEADS * HEAD_DIM
KV_W = KV_HEADS * HEAD_DIM
SSM_HEADS = 24
SSM_P = 64
SSM_DI = SSM_HEADS * SSM_P
SSM_G = 2
SSM_N = 128
SSM_GW = SSM_DI // SSM_G
CONV_K = 4
CONV_DIM = SSM_DI + 2 * SSM_G * SSM_N
N_EXP = 8
EPS = 1e-6
NEG = -1e30

SRC_K = ATTN_W
SRC_V = ATTN_W + KV_W
SRC_Z = ATTN_W + 2 * KV_W
SRC_DT = SRC_Z + SSM_DI + CONV_DIM
KVX_W = 4 * KV_W
OFF_Q = 0
OFF_KVX = ATTN_W
OFF_Z = OFF_KVX + KVX_W
OFF_XBC = OFF_Z + SSM_DI
W_MAIN = OFF_XBC + CONV_DIM

ATTN_QB = 3
MOE_TM = 512
FFN_TF = 512
WEIGHT_CHUNKS = 8
DMA_UNROLL = 8
VMEM_LIMIT = 48 * 1024 * 1024
MOE_VMEM_LIMIT = 56 * 1024 * 1024


def _pick_tile(n, pref):
    t = pref
    while n % t:
        t //= 2
    return t


def _cparams(sem):
    return pltpu.CompilerParams(dimension_semantics=sem, vmem_limit_bytes=VMEM_LIMIT)


def _rms(xf, g):
    return xf * lax.rsqrt(jnp.mean(xf * xf, axis=-1, keepdims=True) + EPS) * g


def _sigmoid(x):
    return 1.0 / (1.0 + jnp.exp(-x))


def _split3(v):
    hi = v.astype(BF16)
    r1 = v - hi.astype(F32)
    mid = r1.astype(BF16)
    lo = (r1 - mid.astype(F32)).astype(BF16)
    return hi, mid, lo


def _norm_kernel(h_ref, g_ref, o_ref):
    o_ref[...] = _rms(h_ref[...], g_ref[...]).astype(o_ref.dtype)


def _norm_call(h, g, out_dtype=None):
    out_dtype = BF16 if out_dtype is None else out_dtype
    t = h.shape[0]
    tm = _pick_tile(t, 512)
    return pl.pallas_call(
        _norm_kernel,
        out_shape=jax.ShapeDtypeStruct((t, D_MODEL), out_dtype),
        grid=(t // tm,),
        in_specs=[pl.BlockSpec((tm, D_MODEL), lambda i: (i, 0)),
                  pl.BlockSpec((1, D_MODEL), lambda i: (0, 0))],
        out_specs=pl.BlockSpec((tm, D_MODEL), lambda i: (i, 0)),
        compiler_params=_cparams(("arbitrary",)),
        name="rmsnorm",
    )(h, g)


def _embed_kernel(*refs):
    *x_refs, head_ref, g_ref, h_ref, xn_ref = refs
    for i, x_ref in enumerate(x_refs):
        rows = slice(BLK * i, BLK * (i + 1))
        h = x_ref[...].astype(F32)
        if i == 0:
            h = jnp.where(pl.program_id(1) == 0, head_ref[...], h)
        h_ref[rows, :] = h
        xn_ref[rows, :] = _rms(h, g_ref[...]).astype(xn_ref.dtype)


def _embed_call(x, meta_tokens, g):
    bsz, seq, _ = x.shape
    nb = seq // BLK + 1
    qb = ATTN_QB if nb % ATTN_QB == 0 else 1
    steps = nb // qb
    t = bsz * nb * BLK
    head = jnp.pad(meta_tokens.astype(F32), ((PAD, 0), (0, 0)))
    tile = lambda b, j: (b * steps + j, 0)
    const = lambda b, j: (0, 0)

    def x_block(i):
        return pl.BlockSpec((BLK, D_MODEL), lambda b, j: (b * (nb - 1) + jnp.maximum(j * qb + i - 1, 0), 0))

    x2d = x.reshape(bsz * seq, D_MODEL)
    return pl.pallas_call(
        _embed_kernel,
        out_shape=(jax.ShapeDtypeStruct((t, D_MODEL), F32),
                   jax.ShapeDtypeStruct((t, D_MODEL), BF16)),
        grid=(bsz, steps),
        in_specs=[x_block(i) for i in range(qb)] + [pl.BlockSpec((BLK, D_MODEL), const),
                                                     pl.BlockSpec((1, D_MODEL), const)],
        out_specs=(pl.BlockSpec((qb * BLK, D_MODEL), tile),
                   pl.BlockSpec((qb * BLK, D_MODEL), tile)),
        compiler_params=_cparams(("arbitrary", "arbitrary")),
        name="embed",
    )(*([x2d] * qb), head, g)


def _inproj_kernel(x_ref, w_ref, wdt_ref, q_ref, kvx_ref, z_ref, xbc_ref, dt_ref):
    x = x_ref[...]

    def mm(lo, n):
        return jnp.dot(x, w_ref[:, lo:lo + n], preferred_element_type=F32)

    q_ref[...] = (mm(OFF_Q, ATTN_W) * (HEAD_DIM ** -0.5)).astype(BF16)
    lane = lax.broadcasted_iota(jnp.int32, (1, KVX_W), 1)
    ones = jnp.where(jnp.logical_and(lane >= 2 * KV_W, (lane & HEAD_DIM) != 0), 1.0, 0.0)
    kvx_ref[...] = (mm(OFF_KVX, KVX_W) + ones).astype(BF16)
    for c in range(SSM_DI // 512):
        z_ref[:, 512 * c:512 * (c + 1)] = mm(OFF_Z + 512 * c, 512).astype(BF16)
    for c in range(CONV_DIM // 512):
        xbc_ref[:, 512 * c:512 * (c + 1)] = mm(OFF_XBC + 512 * c, 512).astype(BF16)
    dt_ref[...] = jnp.dot(x, wdt_ref[...], preferred_element_type=F32)


def _inproj_call(xn, w_main, w_dt):
    t = xn.shape[0]
    tm = _pick_tile(t, 512)
    row = lambda i: (i, 0)
    const = lambda i: (0, 0)
    return pl.pallas_call(
        _inproj_kernel,
        out_shape=(jax.ShapeDtypeStruct((t, ATTN_W), BF16),
                   jax.ShapeDtypeStruct((t, KVX_W), BF16),
                   jax.ShapeDtypeStruct((t, SSM_DI), BF16),
                   jax.ShapeDtypeStruct((t, CONV_DIM), BF16),
                   jax.ShapeDtypeStruct((t, BLK), F32)),
        grid=(t // tm,),
        in_specs=[pl.BlockSpec((tm, D_MODEL), row),
                  pl.BlockSpec((D_MODEL, W_MAIN), const),
                  pl.BlockSpec((D_MODEL, BLK), const)],
        out_specs=(pl.BlockSpec((tm, ATTN_W), row),
                   pl.BlockSpec((tm, KVX_W), row),
                   pl.BlockSpec((tm, SSM_DI), row),
                   pl.BlockSpec((tm, CONV_DIM), row),
                   pl.BlockSpec((tm, BLK), row)),
        compiler_params=_cparams(("arbitrary",)),
        name="in_proj",
    )(xn, w_main, w_dt)


def _pack_w_in(w):
    k0, k1 = w[:, SRC_K:SRC_K + HEAD_DIM], w[:, SRC_K + HEAD_DIM:SRC_V]
    v0, v1 = w[:, SRC_V:SRC_V + HEAD_DIM], w[:, SRC_V + HEAD_DIM:SRC_Z]
    zero = jnp.zeros_like(k0)
    w_main = jnp.concatenate([w[:, :SRC_K], k0, k0, k1, k1, v0, zero, v1, zero, w[:, SRC_Z:SRC_DT]], axis=1)
    w_dt = jnp.pad(w[:, SRC_DT:], ((0, 0), (0, BLK - SSM_HEADS)))
    return w_main.astype(BF16), w_dt.astype(BF16)


def _attn_bias():
    qi = np.arange(BLK)[:, None]
    c = np.arange(3 * BLK)[None, :]
    kb, ki = c // BLK, c % BLK
    out = []
    for n in range(3):
        qpos = n * BLK + qi - PAD
        vis_meta = (ki >= PAD) & (ki - PAD <= qpos)
        vis_prev = (ki > qi) & (n >= 2)
        vis_cur = (ki <= qi) & (n >= 1)
        vis = np.where(kb == 0, vis_meta, np.where(kb == 1, vis_prev, vis_cur))
        out.append(np.where(vis, 0.0, NEG))
    return np.stack(out).astype(np.float32)


def _attn_kernel(sink_ref, q_ref, kvc_ref, kvp_ref, kvm_ref, bias_ref, g_ref, o_ref):
    rep = ATTN_HEADS // KV_HEADS
    lo = lax.broadcasted_iota(jnp.int32, (BLK, BLK), 1) < HEAD_DIM
    srow = lax.broadcasted_iota(jnp.int32, (rep * BLK, 1), 0)
    nt = (((1,), (1,)), ((), ()))
    for i in range(q_ref.shape[0] // BLK):
        rows = slice(BLK * i, BLK * (i + 1))
        before = slice(BLK * (i - 1), BLK * i)
        bias = jnp.concatenate([bias_ref[i]] * rep, axis=0)
        pairs = []
        for g in range(KV_HEADS):
            ksl = slice(BLK * g, BLK * (g + 1))
            vsl = slice(2 * KV_W + BLK * g, 2 * KV_W + BLK * (g + 1))
            qs = []
            for a in range(rep // 2):
                two = q_ref[rows, rep * HEAD_DIM * g + BLK * a:rep * HEAD_DIM * g + BLK * (a + 1)]
                zero = jnp.zeros_like(two)
                qs += [jnp.where(lo, two, zero), jnp.where(lo, zero, two)]
            qs = jnp.concatenate(qs, axis=0)
            k_prev, v_prev = (kvp_ref[:, ksl], kvp_ref[:, vsl]) if i == 0 else (kvc_ref[before, ksl], kvc_ref[before, vsl])
            kcat = jnp.concatenate([kvm_ref[:, ksl], k_prev, kvc_ref[rows, ksl]], axis=0)
            vcat = jnp.concatenate([kvm_ref[:, vsl], v_prev, kvc_ref[rows, vsl]], axis=0)
            s = lax.dot_general(qs, kcat, nt, preferred_element_type=F32) + bias
            sink = sink_ref[rep * g + rep - 1]
            for a in range(rep - 2, -1, -1):
                sink = jnp.where(srow < BLK * (a + 1), sink_ref[rep * g + a], sink)
            m = jnp.maximum(jnp.max(s, axis=-1, keepdims=True), sink)
            p = jnp.exp(s - m).astype(BF16)
            o = jnp.dot(p, vcat, preferred_element_type=F32)
            y = o * pltpu.roll(1.0 / (o + jnp.exp(sink - m)), HEAD_DIM, 1)
            for a in range(rep // 2):
                even = y[2 * a * BLK:(2 * a + 1) * BLK]
                odd = y[(2 * a + 1) * BLK:(2 * a + 2) * BLK]
                pairs.append(jnp.where(lo, even, pltpu.roll(odd, HEAD_DIM, 1)))
        o_ref[rows, :] = _rms(jnp.concatenate(pairs, axis=-1), g_ref[...]).astype(o_ref.dtype)


def _attn_call(q, kvx, sinks, gain, bsz, nb):
    t = q.shape[0]
    qb = ATTN_QB if nb % ATTN_QB == 0 else 1
    steps = nb // qb
    masks = _attn_bias()
    table = np.stack([masks[[min(v * qb + i, 2) for i in range(qb)]] for v in range(3)])
    tile = lambda b, j: (b * steps + j, 0)
    return pl.pallas_call(
        _attn_kernel,
        out_shape=jax.ShapeDtypeStruct((t, ATTN_W), BF16),
        grid=(bsz, steps),
        in_specs=[pl.BlockSpec(memory_space=pltpu.SMEM),
                  pl.BlockSpec((qb * BLK, ATTN_W), tile),
                  pl.BlockSpec((qb * BLK, KVX_W), tile),
                  pl.BlockSpec((BLK, KVX_W), lambda b, j: (b * nb + jnp.maximum(j * qb - 1, 0), 0)),
                  pl.BlockSpec((BLK, KVX_W), lambda b, j: (b * nb, 0)),
                  pl.BlockSpec((None, qb, BLK, 3 * BLK), lambda b, j: (jnp.minimum(j, 2), 0, 0, 0)),
                  pl.BlockSpec((1, ATTN_W), lambda b, j: (0, 0))],
        out_specs=pl.BlockSpec((qb * BLK, ATTN_W), tile),
        compiler_params=_cparams(("arbitrary", "arbitrary")),
        name="swa_attention",
    )(sinks, q, kvx, kvx, kvx, jnp.asarray(table), gain)


def _ssd_kernel(xbc_ref, z_ref, dt_ref, cw_ref, cb_ref, dtb_ref, alog_ref, dsk_ref, nw_ref,
                y_ref, tail_ref, act_ref, state_ref, xd_ref, yd_ref, dout_ref):
    c = pl.program_id(1)

    @pl.when(c == 0)
    def _():
        tail_ref[...] = jnp.zeros_like(tail_ref)
        state_ref[...] = jnp.zeros_like(state_ref)

    valid = jnp.logical_or(c > 0, lax.broadcasted_iota(jnp.int32, (BLK, 1), 0) >= PAD)
    x = jnp.where(valid, xbc_ref[...].astype(F32), 0.0)
    tail_ref[SUBLANES:, :] = x
    acc = x * cw_ref[CONV_K - 1:CONV_K, :] + cb_ref[...]
    for s in range(1, CONV_K):
        acc = acc + tail_ref[SUBLANES - s:SUBLANES - s + BLK, :] * cw_ref[CONV_K - 1 - s:CONV_K - s, :]
    tail_ref[:SUBLANES, :] = x[BLK - SUBLANES:]
    act_ref[...] = acc * _sigmoid(acc)
    bm = act_ref[:, SSM_DI:SSM_DI + SSM_G * SSM_N]
    cm = act_ref[:, SSM_DI + SSM_G * SSM_N:]

    dtr = dt_ref[...] + dtb_ref[...]
    dt = jnp.maximum(dtr, 0.0) + jnp.log1p(jnp.exp(-jnp.abs(dtr)))
    dt = jnp.where(valid, dt, 0.0)
    da = dt * (-jnp.exp(alog_ref[...]))

    row = lax.broadcasted_iota(jnp.int32, (BLK, BLK), 0)
    col = lax.broadcasted_iota(jnp.int32, (BLK, BLK), 1)
    causal = row >= col
    tril = jnp.where(causal, 1.0, 0.0).astype(BF16)
    a_cs = sum(jnp.dot(tril, part, preferred_element_type=F32) for part in _split3(da))
    a_cs_t = a_cs.T
    d_state = jnp.exp(a_cs[BLK - 1:BLK, :] - a_cs)
    d_out = jnp.exp(a_cs)
    dtd = dt * d_state

    nt = (((1,), (1,)), ((), ()))
    cbs = [lax.dot_general(cm[:, SSM_N * g:SSM_N * (g + 1)].astype(BF16),
                           bm[:, SSM_N * g:SSM_N * (g + 1)].astype(BF16),
                           nt, preferred_element_type=F32) for g in range(SSM_G)]
    lo_half = col < SSM_P
    pairs = SSM_HEADS // 2
    for j in range(pairs):
        g = j // (pairs // SSM_G)
        h0, h1 = 2 * j, 2 * j + 1
        sl = slice(BLK * j, BLK * (j + 1))

        def per_head(v):
            return jnp.where(lo_half, v[:, h0:h0 + 1], v[:, h1:h1 + 1])

        xs_p = act_ref[:, sl]
        xdt = xs_p * per_head(dt)
        xd_ref[:, sl] = (xs_p * per_head(dtd)).astype(BF16)
        dout_ref[:, sl] = per_head(d_out)

        def intra(h):
            seg = a_cs[:, h:h + 1] - a_cs_t[h:h + 1, :]
            return (cbs[g] * jnp.exp(jnp.where(causal, seg, NEG))).astype(BF16)

        yd_ref[:, sl] = (
            jnp.dot(intra(h0), jnp.where(lo_half, xdt, 0.0).astype(BF16), preferred_element_type=F32)
            + jnp.dot(intra(h1), jnp.where(lo_half, 0.0, xdt).astype(BF16), preferred_element_type=F32))

    for g in range(SSM_G):
        gs = slice(SSM_GW * g, SSM_GW * (g + 1))
        st = state_ref[g]
        cm_g = cm[:, SSM_N * g:SSM_N * (g + 1)].astype(BF16)
        bm_t = bm[:, SSM_N * g:SSM_N * (g + 1)].T.astype(BF16)
        y_off = jnp.dot(cm_g, st.astype(BF16), preferred_element_type=F32) * dout_ref[:, gs]
        state_ref[g] = (st * dout_ref[BLK - 1:BLK, gs]
                        + jnp.dot(bm_t, xd_ref[:, gs], preferred_element_type=F32))
        y = yd_ref[:, gs] + y_off + act_ref[:, gs] * dsk_ref[:, gs]
        zz = z_ref[:, gs].astype(F32)
        y = y * (zz * _sigmoid(zz))
        y_ref[:, gs] = _rms(y, nw_ref[:, gs]).astype(y_ref.dtype)


def _ssd_call(xbc, z, dt_raw, conv_w, conv_b, dt_bias, a_log, d_skip, norm_w, bsz, nb):
    t = xbc.shape[0]
    blk = lambda b, n: (b * nb + n, 0)
    const = lambda b, n: (0, 0)
    return pl.pallas_call(
        _ssd_kernel,
        out_shape=jax.ShapeDtypeStruct((t, SSM_DI), BF16),
        grid=(bsz, nb),
        in_specs=[pl.BlockSpec((BLK, CONV_DIM), blk),
                  pl.BlockSpec((BLK, SSM_DI), blk),
                  pl.BlockSpec((BLK, BLK), blk),
                  pl.BlockSpec((CONV_K, CONV_DIM), const),
                  pl.BlockSpec((1, CONV_DIM), const),
                  pl.BlockSpec((1, BLK), const),
                  pl.BlockSpec((1, BLK), const),
                  pl.BlockSpec((1, SSM_DI), const),
                  pl.BlockSpec((1, SSM_DI), const)],
        out_specs=pl.BlockSpec((BLK, SSM_DI), blk),
        scratch_shapes=[pltpu.VMEM((SUBLANES + BLK, CONV_DIM), F32),
                        pltpu.VMEM((BLK, CONV_DIM), F32),
                        pltpu.VMEM((SSM_G, SSM_N, SSM_GW), F32),
                        pltpu.VMEM((BLK, SSM_DI), BF16),
                        pltpu.VMEM((BLK, SSM_DI), F32),
                        pltpu.VMEM((BLK, SSM_DI), F32)],
        compiler_params=_cparams(("arbitrary", "arbitrary")),
        name="conv_ssd",
    )(xbc, z, dt_raw, conv_w, conv_b, dt_bias, a_log, d_skip, norm_w)


def _outproj_kernel(ya_ref, ys_ref, h_ref, wa_ref, ws_ref, g_ref, *rest):
    hn = (h_ref[...]
          + jnp.dot(ya_ref[...], wa_ref[...], preferred_element_type=F32)
          + jnp.dot(ys_ref[...], ws_ref[...], preferred_element_type=F32))
    u = _rms(hn, g_ref[...])
    if len(rest) == 2:
        hn_ref, u_ref = rest
    else:
        rhi_ref, rlo_ref, earlier_ref, hn_ref, u_ref, route_ref, cnt_ref = rest
        _route_rows(u, rhi_ref, rlo_ref, earlier_ref, route_ref, cnt_ref)
    hn_ref[...] = hn
    u_ref[...] = u.astype(u_ref.dtype)


def _outproj_call(ya, ys, h, wa, ws, g, u_dtype, router=None):
    t = h.shape[0]
    tm = _pick_tile(t, 512)
    row = lambda i: (i, 0)
    const = lambda i: (0, 0)
    operands = [ya, ys, h, wa, ws, g]
    in_specs = [pl.BlockSpec((tm, ATTN_W), row),
                pl.BlockSpec((tm, SSM_DI), row),
                pl.BlockSpec((tm, D_MODEL), row),
                pl.BlockSpec((ATTN_W, D_MODEL), const),
                pl.BlockSpec((SSM_DI, D_MODEL), const),
                pl.BlockSpec((1, D_MODEL), const)]
    out_shape = [jax.ShapeDtypeStruct((t, D_MODEL), F32), jax.ShapeDtypeStruct((t, D_MODEL), u_dtype)]
    out_specs = [pl.BlockSpec((tm, D_MODEL), row), pl.BlockSpec((tm, D_MODEL), row)]
    if router is not None:
        earlier = jnp.asarray(np.tril(np.ones((tm, tm), np.float32), -1), BF16)
        operands += list(_router_weights(router)) + [earlier]
        in_specs += [pl.BlockSpec((D_MODEL, BLK), const), pl.BlockSpec((D_MODEL, BLK), const),
                     pl.BlockSpec((tm, tm), const)]
        out_shape += [jax.ShapeDtypeStruct((t, BLK), F32), jax.ShapeDtypeStruct((1, BLK), F32)]
        out_specs += [pl.BlockSpec((tm, BLK), row), pl.BlockSpec((1, BLK), const)]
    return pl.pallas_call(
        _outproj_kernel,
        out_shape=tuple(out_shape),
        grid=(t // tm,),
        in_specs=in_specs,
        out_specs=tuple(out_specs),
        compiler_params=_cparams(("arbitrary",)),
        name="out_proj",
    )(*operands)


def _swiglu_tile(x, w1_ref, w3_ref, w2_ref, act_ref):
    d_ff = w1_ref.shape[1]
    tf = _pick_tile(d_ff, FFN_TF)
    for c in range(d_ff // tf):
        sl = slice(tf * c, tf * (c + 1))
        a = jnp.dot(x, w1_ref[:, sl], preferred_element_type=F32)
        b = jnp.dot(x, w3_ref[:, sl], preferred_element_type=F32)
        act_ref[:, sl] = (a * _sigmoid(a) * b).astype(BF16)
    return jnp.dot(act_ref[...], w2_ref[...], preferred_element_type=F32)


def _ffn_kernel(u_ref, h_ref, w1_ref, w3_ref, w2_ref, g_ref, hn_ref, xn_ref, act_ref):
    hn = h_ref[...] + _swiglu_tile(u_ref[...], w1_ref, w3_ref, w2_ref, act_ref)
    hn_ref[...] = hn
    xn_ref[...] = _rms(hn, g_ref[...]).astype(xn_ref.dtype)


def _ffn_call(u, h, w1, w3, w2, g):
    t = h.shape[0]
    d_ff = w1.shape[1]
    tm = _pick_tile(t, 512)
    row = lambda i: (i, 0)
    const = lambda i: (0, 0)
    once = pl.Buffered(1)
    return pl.pallas_call(
        _ffn_kernel,
        out_shape=(jax.ShapeDtypeStruct((t, D_MODEL), F32),
                   jax.ShapeDtypeStruct((t, D_MODEL), BF16)),
        grid=(t // tm,),
        in_specs=[pl.BlockSpec((tm, D_MODEL), row),
                  pl.BlockSpec((tm, D_MODEL), row),
                  pl.BlockSpec((D_MODEL, d_ff), const, pipeline_mode=once),
                  pl.BlockSpec((D_MODEL, d_ff), const, pipeline_mode=once),
                  pl.BlockSpec((d_ff, D_MODEL), const, pipeline_mode=once),
                  pl.BlockSpec((1, D_MODEL), const)],
        out_specs=(pl.BlockSpec((tm, D_MODEL), row),
                   pl.BlockSpec((tm, D_MODEL), row)),
        scratch_shapes=[pltpu.VMEM((tm, d_ff), BF16)],
        compiler_params=_cparams(("arbitrary",)),
        name="dense_ffn",
    )(u, h, w1, w3, w2, g)


R_E1, R_E2, R_G1, R_G2, R_K1, R_K2 = range(6)


def _route_rows(u, rhi_ref, rlo_ref, earlier_ref, route_ref, cnt_ref):
    tm = u.shape[0]

    @pl.when(pl.program_id(0) == 0)
    def _():
        cnt_ref[...] = jnp.zeros_like(cnt_ref)

    u_hi = u.astype(BF16)
    u_lo = (u - u_hi.astype(F32)).astype(BF16)
    logits = (jnp.dot(u_hi, rhi_ref[...], preferred_element_type=F32)
              + jnp.dot(u_lo, rhi_ref[...], preferred_element_type=F32)
              + jnp.dot(u_hi, rlo_ref[...], preferred_element_type=F32))
    lane = lax.broadcasted_iota(jnp.int32, (tm, BLK), 1)
    logits = jnp.where(lane < N_EXP, logits, -jnp.inf)
    m1 = jnp.max(logits, axis=-1, keepdims=True)
    i1 = jnp.min(jnp.where(logits == m1, lane, BLK), axis=-1, keepdims=True)
    rest = jnp.where(lane == i1, -jnp.inf, logits)
    m2 = jnp.max(rest, axis=-1, keepdims=True)
    i2 = jnp.min(jnp.where(rest == m2, lane, BLK), axis=-1, keepdims=True)
    e21 = jnp.exp(m2 - m1)
    g1 = 1.0 / (1.0 + e21)
    g2 = e21 / (1.0 + e21)

    sel1 = lane == i1
    sel2 = lane == i2
    onehot = jnp.where(sel1, 1.0, 0.0) + jnp.where(sel2, 1.0, 0.0)
    before = (jnp.dot(earlier_ref[...], onehot.astype(BF16), preferred_element_type=F32) + cnt_ref[...])
    k1 = jnp.sum(jnp.where(sel1, before, 0.0), axis=-1, keepdims=True)
    k2 = jnp.sum(jnp.where(sel2, before, 0.0), axis=-1, keepdims=True)
    cnt_ref[...] += jnp.sum(onehot, axis=0, keepdims=True)

    rec = jnp.zeros((tm, BLK), F32)
    for idx, val in ((R_E1, i1.astype(F32)), (R_E2, i2.astype(F32)), (R_G1, g1), (R_G2, g2),
                     (R_K1, k1), (R_K2, k2)):
        rec = jnp.where(lane == idx, val, rec)
    route_ref[...] = rec


def _router_weights(router):
    r_pad = jnp.zeros((D_MODEL, BLK), F32).at[:, :N_EXP].set(router)
    r_hi = r_pad.astype(BF16)
    return r_hi, (r_pad - r_hi.astype(F32)).astype(BF16)


def _dispatch_kernel(zt_ref, pos_ref, u_ref, xs_hbm, zero_ref, stage_ref, sem, zsem, *, tm):
    n_fill = zt_ref.shape[0]
    step = pl.program_id(0)
    slot = lax.rem(step, 2)

    def fill_copy(j):
        return pltpu.make_async_copy(zero_ref, xs_hbm.at[pl.ds(pl.multiple_of(zt_ref[j], MOE_TM), MOE_TM)], zsem)

    @pl.when(pl.program_id(0) == 0)
    def _():
        zero_ref[...] = jnp.zeros_like(zero_ref)
        for j in range(n_fill):
            fill_copy(j).start()
            fill_copy(j).wait()

    stage_ref[slot] = u_ref[...]

    def issue(t, carry):
        for k in range(2):
            pltpu.make_async_copy(stage_ref.at[slot, pl.ds(t, 1)],
                                  xs_hbm.at[pl.ds(pos_ref[2 * t + k], 1)], sem.at[slot, k]).start(priority=k)
        return carry

    lax.fori_loop(0, tm, issue, 0, unroll=DMA_UNROLL)

    def drain(s):
        for k in range(2):
            pltpu.make_async_copy(stage_ref.at[s], xs_hbm.at[pl.ds(0, tm)], sem.at[s, k]).wait()

    @pl.when(step > 0)
    def _():
        drain(1 - slot)

    @pl.when(step == pl.num_programs(0) - 1)
    def _():
        drain(slot)


def _dispatch_call(fill_tiles, pos_flat, u, r_max):
    t = u.shape[0]
    tm = _pick_tile(t, 512)
    grid_spec = pltpu.PrefetchScalarGridSpec(
        num_scalar_prefetch=1,
        grid=(t // tm,),
        in_specs=[pl.BlockSpec((2 * tm,), lambda i, zt: (i,), memory_space=pltpu.SMEM),
                  pl.BlockSpec((tm, D_MODEL), lambda i, zt: (i, 0))],
        out_specs=pl.BlockSpec(memory_space=pl.ANY),
        scratch_shapes=[pltpu.VMEM((MOE_TM, D_MODEL), F32),
                        pltpu.VMEM((2, tm, D_MODEL), F32),
                        pltpu.SemaphoreType.DMA((2, 2)),
                        pltpu.SemaphoreType.DMA(())],
    )
    return pl.pallas_call(
        functools.partial(_dispatch_kernel, tm=tm),
        out_shape=jax.ShapeDtypeStruct((r_max, D_MODEL), F32),
        grid_spec=grid_spec,
        compiler_params=pltpu.CompilerParams(dimension_semantics=("arbitrary",),
                                             vmem_limit_bytes=VMEM_LIMIT, has_side_effects=True),
        name="moe_dispatch",
    )(fill_tiles, pos_flat, u)


def _moe_kernel(te_ref, nv_ref, x_ref, w1_hbm, w3_hbm, w2_hbm, y_ref,
                w1_ref, w3_ref, w2_ref, up_stage, down_stage, act_ref, sem, *, layer):
    i = pl.program_id(0)
    e = te_ref[i]
    d_ff = w1_ref.shape[1]
    up_rows, down_rows = up_stage.shape[1], down_stage.shape[1]
    plan = ([(w1_hbm, w1_ref, up_stage, r) for r in range(0, D_MODEL, up_rows)]
            + [(w3_hbm, w3_ref, up_stage, r) for r in range(0, D_MODEL, up_rows)]
            + [(w2_hbm, w2_ref, down_stage, r) for r in range(0, d_ff, down_rows)])

    def chunk_copy(j):
        src, _, stage, r0 = plan[j]
        return pltpu.make_async_copy(src.at[layer, e, pl.ds(r0, stage.shape[1])], stage.at[j % 2], sem.at[j % 2])

    @pl.when(jnp.logical_or(i == 0, e != te_ref[jnp.maximum(i - 1, 0)]))
    def _():
        chunk_copy(0).start()
        for j in range(len(plan)):
            if j + 1 < len(plan):
                chunk_copy(j + 1).start()
            chunk_copy(j).wait()
            _, home, stage, r0 = plan[j]
            home[r0:r0 + stage.shape[1], :] = stage[j % 2].astype(BF16)

    @pl.when(i < nv_ref[0])
    def _():
        y_ref[...] = _swiglu_tile(x_ref[...].astype(BF16), w1_ref, w3_ref, w2_ref, act_ref)

    @pl.when(i >= nv_ref[0])
    def _():
        y_ref[...] = jnp.zeros_like(y_ref)


def _moe_call(tile_expert, n_valid, xs, w1, w3, w2, layer):
    r_max = xs.shape[0]
    d_ff = w1.shape[3]
    tm = MOE_TM
    grid_spec = pltpu.PrefetchScalarGridSpec(
        num_scalar_prefetch=2,
        grid=(r_max // tm,),
        in_specs=[pl.BlockSpec((tm, D_MODEL), lambda i, te, nv: (i, 0)),
                  pl.BlockSpec(memory_space=pl.ANY),
                  pl.BlockSpec(memory_space=pl.ANY),
                  pl.BlockSpec(memory_space=pl.ANY)],
        out_specs=pl.BlockSpec((tm, D_MODEL), lambda i, te, nv: (i, 0)),
        scratch_shapes=[pltpu.VMEM((D_MODEL, d_ff), BF16),
                        pltpu.VMEM((D_MODEL, d_ff), BF16),
                        pltpu.VMEM((d_ff, D_MODEL), BF16),
                        pltpu.VMEM((2, D_MODEL // WEIGHT_CHUNKS, d_ff), F32),
                        pltpu.VMEM((2, d_ff // WEIGHT_CHUNKS, D_MODEL), F32),
                        pltpu.VMEM((tm, d_ff), BF16),
                        pltpu.SemaphoreType.DMA((2,))],
    )
    return pl.pallas_call(
        functools.partial(_moe_kernel, layer=layer),
        out_shape=jax.ShapeDtypeStruct((r_max, D_MODEL), F32),
        grid_spec=grid_spec,
        compiler_params=pltpu.CompilerParams(dimension_semantics=("arbitrary",),
                                             vmem_limit_bytes=MOE_VMEM_LIMIT),
        name="moe_ffn",
    )(tile_expert, n_valid, xs, w1, w3, w2)


def _combine_kernel(pos_ref, pos_next_ref, h_ref, route_ref, g_ref, ys_hbm, *rest, tm, emit_h):
    if emit_h:
        hn_ref, xn_ref, y_ref, sem = rest
    else:
        xn_ref, y_ref, sem = rest
    step = pl.program_id(0)
    n_steps = pl.num_programs(0)
    if not emit_h:
        step = step * pl.num_programs(1) + pl.program_id(1)
        n_steps = n_steps * pl.num_programs(1)
    slot = lax.rem(step, 2)

    def gather(table_ref, into):
        def issue(t, carry):
            for k in range(2):
                pltpu.make_async_copy(ys_hbm.at[pl.ds(table_ref[2 * t + k], 1)],
                                      y_ref.at[into, k, pl.ds(t, 1)], sem.at[into, k]).start(priority=k)
            return carry
        lax.fori_loop(0, tm, issue, 0, unroll=DMA_UNROLL)

    @pl.when(step == 0)
    def _():
        gather(pos_ref, 0)

    @pl.when(step + 1 < n_steps)
    def _():
        gather(pos_next_ref, 1 - slot)

    for k in range(2):
        pltpu.make_async_copy(ys_hbm.at[pl.ds(0, tm)], y_ref.at[slot, k], sem.at[slot, k]).wait()
    hn = (h_ref[...]
          + route_ref[:, R_G1:R_G1 + 1] * y_ref[slot, 0]
          + route_ref[:, R_G2:R_G2 + 1] * y_ref[slot, 1])
    if emit_h:
        hn_ref[...] = hn
    xn_ref[...] = _rms(hn, g_ref[...]).astype(xn_ref.dtype)


def _combine_call(pos_flat, h, route, g, ys, emit_h, bsz, nb):
    t = h.shape[0]
    if emit_h:
        tm = _pick_tile(t, 256)
        grid = (t // tm,)
        row = lambda i: (i, 0)
        flat = lambda i: (i,)
        flat_next = lambda i: (jnp.minimum(i + 1, t // tm - 1),)
        const = lambda i: (0, 0)
        out_shape = (jax.ShapeDtypeStruct((t, D_MODEL), F32), jax.ShapeDtypeStruct((t, D_MODEL), BF16))
        out_specs = (pl.BlockSpec((tm, D_MODEL), row), pl.BlockSpec((tm, D_MODEL), row))
    else:
        tm = BLK
        grid = (bsz, nb)
        row = lambda b, n: (b * nb + n, 0)
        flat = lambda b, n: (b * nb + n,)
        flat_next = lambda b, n: (jnp.minimum(b * nb + n + 1, bsz * nb - 1),)
        const = lambda b, n: (0, 0)
        out_shape = jax.ShapeDtypeStruct((bsz * (nb - 1) * BLK, D_MODEL), F32)
        out_specs = pl.BlockSpec((tm, D_MODEL), lambda b, n: (b * (nb - 1) + jnp.maximum(n - 1, 0), 0))
    return pl.pallas_call(
        functools.partial(_combine_kernel, tm=tm, emit_h=emit_h),
        out_shape=out_shape,
        grid=grid,
        in_specs=[pl.BlockSpec((2 * tm,), flat, memory_space=pltpu.SMEM),
                  pl.BlockSpec((2 * tm,), flat_next, memory_space=pltpu.SMEM),
                  pl.BlockSpec((tm, D_MODEL), row),
                  pl.BlockSpec((tm, BLK), row),
                  pl.BlockSpec((1, D_MODEL), const),
                  pl.BlockSpec(memory_space=pl.ANY)],
        out_specs=out_specs,
        scratch_shapes=[pltpu.VMEM((2, 2, tm, D_MODEL), F32),
                        pltpu.SemaphoreType.DMA((2, 2))],
        compiler_params=_cparams(("arbitrary",) * len(grid)),
        name="moe_combine",
    )(pos_flat, pos_flat, h, route, g, ys)


def _moe_layer(h, u, route, counts, w1, w3, w2, layer, g_next, emit_h, bsz, nb):
    t = h.shape[0]
    tm = MOE_TM
    n_tiles = -(-(2 * t + N_EXP * (tm - 1)) // tm)
    r_max = n_tiles * tm

    cnt = counts[0, :N_EXP].astype(jnp.int32)
    padded = (cnt + tm - 1) // tm * tm
    ends = jnp.cumsum(padded)
    starts = ends - padded
    experts = route[:, R_E1:R_E2 + 1].astype(jnp.int32)
    ranks = route[:, R_K1:R_K2 + 1].astype(jnp.int32)
    pos = (starts[experts] + ranks).reshape(-1)
    n_valid = (ends[N_EXP - 1] // tm).astype(jnp.int32).reshape(1)
    tile_start = jnp.minimum(jnp.arange(n_tiles, dtype=jnp.int32), n_valid[0] - 1) * tm
    tile_expert = jnp.sum(ends[None, :] <= tile_start[:, None], axis=1).astype(jnp.int32)
    idle = ends[N_EXP - 1] + jnp.arange(N_EXP, dtype=jnp.int32) * tm
    fill_tiles = jnp.clip(jnp.concatenate([ends - tm, idle]), 0, r_max - tm).astype(jnp.int32)

    xs = _dispatch_call(fill_tiles, pos, u, r_max)
    ys = _moe_call(tile_expert, n_valid, xs, w1, w3, w2, layer)
    return _combine_call(pos, h, route, g_next, ys, emit_h, bsz, nb)


def kernel(x, meta_tokens, norm_mix, w_in, conv_w, conv_b, dt_bias, a_log, d_skip, ssm_norm,
           attn_norm, sinks, w_out, norm_ffn, ffn_w1, ffn_w3, ffn_w2, router, moe_w1, moe_w3,
           moe_w2, final_norm):
    bsz, seq, _ = x.shape
    depth = w_in.shape[0]
    assert seq % BLK == 0
    nb = seq // BLK + 1
    lp = nb * BLK

    def row(v, width=None):
        v = v.astype(F32).reshape(1, -1)
        if width is not None:
            v = jnp.pad(v, ((0, 0), (0, width - v.shape[1])))
        return v

    h, xn = _embed_call(x, meta_tokens, row(norm_mix[0]))
    out = None
    for i in range(depth):
        w_main, w_dt = _pack_w_in(w_in[i])
        q, kvx, z, xbc, dt_raw = _inproj_call(xn, w_main, w_dt)
        y_attn = _attn_call(q, kvx, sinks[i].astype(F32), row(attn_norm[i]), bsz, nb)
        y_ssm = _ssd_call(xbc, z, dt_raw, conv_w[i].T.astype(F32), row(conv_b[i]),
                          row(dt_bias[i], BLK), row(a_log[i], BLK),
                          row(jnp.repeat(d_skip[i], SSM_P)), row(ssm_norm[i]), bsz, nb)
        wo = w_out[i].astype(BF16)
        last = i == depth - 1
        g_next = row(final_norm) if last else row(norm_mix[i + 1])
        j = i // 2
        if i % 2 == 0:
            h, u = _outproj_call(y_attn, y_ssm, h, wo[:ATTN_W], wo[ATTN_W:], row(norm_ffn[i]), BF16)
            h, xn = _ffn_call(u, h, ffn_w1[j].astype(BF16), ffn_w3[j].astype(BF16),
                              ffn_w2[j].astype(BF16), g_next)
            if last:
                out = _norm_call(h, g_next, F32)
        else:
            h, u, route, counts = _outproj_call(y_attn, y_ssm, h, wo[:ATTN_W], wo[ATTN_W:],
                                                row(norm_ffn[i]), F32, router[j])
            res = _moe_layer(h, u, route, counts, moe_w1.astype(F32), moe_w3.astype(F32),
                             moe_w2.astype(F32), j, g_next, not last, bsz, nb)
            if last:
                return res.reshape(bsz, seq, D_MODEL).astype(x.dtype)
            h, xn = res
    return out.reshape(bsz, lp, D_MODEL)[:, BLK:].astype(x.dtype)
```
